```python
import math
import jax, jax.numpy as jnp
from jax import lax
import numpy as np

D_MODEL = 1024
BATCH = 2
SEQ = 16384
DEPTH = 2

GDN_HEADS = 8
GDN_HEAD_DIM = 64
GDN_WIDTH = GDN_HEADS * GDN_HEAD_DIM
GDN_CONV = 4
GDN_CHUNK = 64
MLA_HEADS = 8
MLA_Q_RANK = 256
MLA_KV_RANK = 128
MLA_NOPE = 64
MLA_ROPE = 32
MLA_QK = MLA_NOPE + MLA_ROPE
MLA_V = 64
MLA_WIDTH = MLA_HEADS * MLA_V
ROPE_THETA = 10000.0
Q_BLOCK = 128
NEG_INF = -1e30
IN_SPLITS = (GDN_WIDTH, GDN_WIDTH, GDN_WIDTH, GDN_WIDTH, GDN_HEADS, GDN_HEADS, MLA_Q_RANK, MLA_KV_RANK, MLA_ROPE)
IN_COLS = sum(IN_SPLITS)
MIX_WIDTH = GDN_WIDTH + MLA_WIDTH
RWKV_HEADS = 16
RWKV_HEAD_DIM = 64
DECAY_LORA = 64
AAA_LORA = 64
GATE_LORA = 160
RWKV_GN_EPS = 64e-5
D_FF = 4 * D_MODEL
EPS = 1e-6

kernel_name = "hybrid_gdn_mla_rwkv7_adaln"


def rms_norm(t):
    tf = t.astype(jnp.float32)
    return (tf * lax.rsqrt(jnp.mean(tf * tf, axis=-1, keepdims=True) + EPS)).astype(t.dtype)


def l2_normalize(t):
    tf = t.astype(jnp.float32)
    return (tf * lax.rsqrt(jnp.sum(tf * tf, axis=-1, keepdims=True) + 1e-12)).astype(t.dtype)


def adaln(t, shift, scale):
    return rms_norm(t) * (1 + scale[:, None, :]) + shift[:, None, :]


def split_columns(t, sizes):
    offsets = []
    total = 0
    for s in sizes[:-1]:
        total += s
        offsets.append(total)
    return jnp.split(t, offsets, axis=-1)


def causal_depthwise_conv(t, w):
    K = w.shape[0]
    S = t.shape[1]
    tp = jnp.pad(t, ((0, 0), (K - 1, 0), (0, 0)))
    out = tp[:, 0:S] * w[0]
    for j in range(1, K):
        out = out + tp[:, j:j + S] * w[j]
    return out


def rope_tables(positions):
    inv_freq = ROPE_THETA ** (-jnp.arange(0, MLA_ROPE, 2, dtype=jnp.float32) / MLA_ROPE)
    ang = positions.astype(jnp.float32)[..., None] * inv_freq
    return jnp.cos(ang)[:, :, None, :], jnp.sin(ang)[:, :, None, :]


def apply_rope_tail(t, cos, sin):
    t_pass, t_rot = t[..., :MLA_NOPE], t[..., MLA_NOPE:]
    x1, x2 = jnp.split(t_rot.astype(jnp.float32), 2, axis=-1)
    rot = jnp.concatenate([x1 * cos - x2 * sin, x2 * cos + x1 * sin], axis=-1).astype(t.dtype)
    return jnp.concatenate([t_pass, rot], axis=-1)


def chunked_gated_delta_rule(q, k, v, g, beta):
    B, S, H, Dk = q.shape
    Dv = v.shape[-1]
    C = GDN_CHUNK
    N = S // C
    f32 = jnp.float32

    def to_chunks(t):
        t = t.astype(f32).reshape((B, N, C, H) + t.shape[3:])
        return jnp.moveaxis(t, 3, 1)

    q = to_chunks(q) * (Dk ** -0.5)
    k = to_chunks(k)
    v = to_chunks(v)
    beta = to_chunks(beta)
    g = jnp.cumsum(to_chunks(g), axis=-1)
    causal = jnp.tril(jnp.ones((C, C), dtype=bool))
    strict = jnp.tril(jnp.ones((C, C), dtype=bool), -1)
    gdiff = g[..., :, None] - g[..., None, :]
    decay = jnp.where(causal, jnp.exp(jnp.where(causal, gdiff, 0.0)), 0.0)
    k_beta = k * beta[..., None]
    L = jnp.where(strict, jnp.einsum('bhnid,bhnjd->bhnij', k_beta, k) * decay, 0.0)
    eye = jnp.eye(C, dtype=f32)
    T = lax.linalg.triangular_solve(eye + L, jnp.broadcast_to(eye, L.shape),
                                    left_side=True, lower=True, unit_diagonal=True)
    u = T @ (v * beta[..., None])
    w = T @ (k_beta * jnp.exp(g)[..., None])
    qk = jnp.where(causal, jnp.einsum('bhnid,bhnjd->bhnij', q, k) * decay, 0.0)
    g_last = g[..., -1]
    q_dec = q * jnp.exp(g)[..., None]
    k_dec = k * jnp.exp(g_last[..., None] - g)[..., None]

    def step(state, inp):
        q_i, k_i, w_i, u_i, qk_i, gl_i = inp
        v_new = u_i - w_i @ state
        o_i = q_i @ state + qk_i @ v_new
        state = state * jnp.exp(gl_i)[..., None, None] + jnp.swapaxes(k_i, -1, -2) @ v_new
        return state, o_i

    xs = tuple(jnp.moveaxis(t, 2, 0) for t in (q_dec, k_dec, w, u, qk, g_last))
    state0 = jnp.zeros((B, H, Dk, Dv), f32)
    _, o = lax.scan(step, state0, xs)
    return o.transpose(1, 0, 3, 2, 4).reshape(B, S, H, Dv)


def causal_block_attention(q, k, v):
    B, S, H, Dqk = q.shape
    Dv = v.shape[-1]
    nb = S // Q_BLOCK
    q_blocks = jnp.moveaxis(q.reshape(B, nb, Q_BLOCK, H, Dqk), 1, 0)
    key_pos = jnp.arange(S)
    scale = Dqk ** -0.5

    def one_block(args):
        q_blk, blk = args
        s = jnp.einsum('bqhd,bkhd->bhqk', q_blk, k, preferred_element_type=jnp.float32) * scale
        query_pos = blk * Q_BLOCK + jnp.arange(Q_BLOCK)
        s = jnp.where(key_pos[None, :] <= query_pos[:, None], s, NEG_INF)
        p = jax.nn.softmax(s, axis=-1).astype(v.dtype)
        return jnp.einsum('bhqk,bkhd->bqhd', p, v)

    o = lax.map(one_block, (q_blocks, jnp.arange(nb)))
    return jnp.moveaxis(o, 0, 1).reshape(B, S, H, Dv)


def gdn_mla_mixer(h, cos, sin, w_in, conv_w, a_log, dt_bias, gdn_gain, q_norm_gain, kv_norm_gain,
                  w_uq, w_ukv, q_head_gain, k_head_gain, w_out):
    B, S, _ = h.shape
    proj = h @ w_in
    q, k, v, z, a, b, dq, dkv, k_rope = split_columns(proj, IN_SPLITS)
    qkv = jax.nn.silu(causal_depthwise_conv(jnp.concatenate([q, k, v], axis=-1), conv_w))
    q, k, v = jnp.split(qkv, 3, axis=-1)
    gshape = (B, S, GDN_HEADS, GDN_HEAD_DIM)
    q = l2_normalize(q.reshape(gshape))
    k = l2_normalize(k.reshape(gshape))
    v = v.reshape(gshape)
    beta = jax.nn.sigmoid(b.astype(jnp.float32))
    g = -jnp.exp(a_log.astype(jnp.float32)) * jax.nn.softplus(a.astype(jnp.float32) + dt_bias)
    o = chunked_gated_delta_rule(q, k, v, g, beta).astype(h.dtype)
    o_gdn = (rms_norm(o) * gdn_gain * jax.nn.silu(z.reshape(gshape))).reshape(B, S, GDN_WIDTH)
    q_lat = rms_norm(dq) * q_norm_gain
    qm = (q_lat @ w_uq).reshape(B, S, MLA_HEADS, MLA_QK)
    kv_lat = rms_norm(dkv) * kv_norm_gain
    kv = (kv_lat @ w_ukv).reshape(B, S, MLA_HEADS, MLA_NOPE + MLA_V)
    k_nope, vm = kv[..., :MLA_NOPE], kv[..., MLA_NOPE:]
    k_rope_h = jnp.broadcast_to(k_rope[:, :, None, :], (B, S, MLA_HEADS, MLA_ROPE))
    km = jnp.concatenate([k_nope, k_rope_h], axis=-1)
    qm = apply_rope_tail(rms_norm(qm) * q_head_gain, cos, sin)
    km = apply_rope_tail(rms_norm(km) * k_head_gain, cos, sin)
    o_mla = causal_block_attention(qm, km, vm).reshape(B, S, MLA_WIDTH)
    return jnp.concatenate([o_gdn, o_mla], axis=-1) @ w_out


def rwkv7_recurrence(r, w, k, v, a, b):
    B, S, H, N = r.shape

    def step(state, inp):
        r_t, w_t, k_t, v_t, a_t, b_t = inp
        sa = jnp.einsum('bhvk,bhk->bhv', state, a_t)
        state = (state * w_t[:, :, None, :] + sa[..., None] * b_t[:, :, None, :]
                 + v_t[..., None] * k_t[:, :, None, :])
        return state, jnp.einsum('bhvk,bhk->bhv', state, r_t)

    xs = tuple(jnp.moveaxis(t.astype(jnp.float32), 1, 0) for t in (r, w, k, v, a, b))
    _, y = lax.scan(step, jnp.zeros((B, H, N, N), jnp.float32), xs)
    return jnp.moveaxis(y, 0, 1)


def rwkv7_time_mix(h, mu, w_r, w_k, w_v, w_o, w0, w1, w2, a0, a1, a2, g1, g2,
                   k_k, k_a, r_k, ln_gain, ln_bias):
    B, S, D = h.shape
    hshape = (B, S, RWKV_HEADS, RWKV_HEAD_DIM)
    xx = jnp.pad(h[:, :-1], ((0, 0), (1, 0), (0, 0))) - h
    xr, xw, xk, xv, xa, xg = [h + xx * mu[j] for j in range(6)]
    r = xr @ w_r
    k = xk @ w_k
    v = xv @ w_v
    w_log = -jax.nn.softplus(-(w0 + jnp.tanh(xw @ w1) @ w2).astype(jnp.float32)) - 0.5
    decay = jnp.exp(-jnp.exp(w_log))
    a = jax.nn.sigmoid(a0 + (xa @ a1) @ a2)
    g = jax.nn.sigmoid(xg @ g1) @ g2
    kk = l2_normalize((k * k_k).reshape(hshape))
    k = k * (1 + (a - 1) * k_a)
    r_h, k_h, v_h, a_h = r.reshape(hshape), k.reshape(hshape), v.reshape(hshape), a.reshape(hshape)
    y = rwkv7_recurrence(r_h, decay.reshape(hshape), k_h, v_h, -kk, kk * a_h)
    mean = jnp.mean(y, axis=-1, keepdims=True)
    var = jnp.mean(jnp.square(y - mean), axis=-1, keepdims=True)
    y = ((y - mean) * lax.rsqrt(var + RWKV_GN_EPS)).reshape(B, S, D) * ln_gain + ln_bias
    bonus = jnp.sum((r_h * k_h * r_k).astype(jnp.float32), axis=-1, keepdims=True) * v_h
    y = (y + bonus.reshape(B, S, D)) * g
    return y.astype(h.dtype) @ w_o


def squared_relu_mlp(h, w_up, w_down):
    return jnp.square(jax.nn.relu(h @ w_up)) @ w_down


def setup_inputs(seed: int = 0) -> dict:
    key = jax.random.key(seed)
    keys = iter(jax.random.split(key, 48))
    f32 = jnp.float32
    n_even = (DEPTH + 1) // 2
    n_odd = DEPTH // 2

    def nrm(shape, s):
        return jax.random.normal(next(keys), shape, f32) * s

    def uni(shape, lo, hi):
        return jax.random.uniform(next(keys), shape, f32, minval=lo, maxval=hi)

    x = nrm((BATCH, SEQ, D_MODEL), 1.0)
    c = nrm((BATCH, D_MODEL), 1.0)
    positions = (jnp.arange(SEQ, dtype=jnp.int32)[None, :]
                 + jax.random.randint(next(keys), (BATCH, 1), 0, 4096, dtype=jnp.int32))
    dt = jnp.exp(uni((n_even, GDN_HEADS), math.log(1e-3), math.log(1e-1)))
    return {
        "x": x,
        "c": c,
        "positions": positions,
        "w_mod": nrm((DEPTH, D_MODEL, 6 * D_MODEL), D_MODEL ** -0.5),
        "b_mod": nrm((DEPTH, 6 * D_MODEL), 0.01),
        "w_in0": nrm((n_even, D_MODEL, IN_COLS), D_MODEL ** -0.5),
        "gdn_conv_w": nrm((n_even, GDN_CONV, 3 * GDN_WIDTH), GDN_CONV ** -0.5),
        "gdn_a_log": jnp.log(uni((n_even, GDN_HEADS), 1.0, 16.0)),
        "gdn_dt_bias": dt + jnp.log(-jnp.expm1(-dt)),
        "gdn_norm_gain": 1.0 + nrm((n_even, GDN_HEAD_DIM), 0.02),
        "mla_q_norm_gain": 1.0 + nrm((n_even, MLA_Q_RANK), 0.02),
        "mla_kv_norm_gain": 1.0 + nrm((n_even, MLA_KV_RANK), 0.02),
        "mla_w_uq": nrm((n_even, MLA_Q_RANK, MLA_HEADS * MLA_QK), MLA_Q_RANK ** -0.5),
        "mla_w_ukv": nrm((n_even, MLA_KV_RANK, MLA_HEADS * (MLA_NOPE + MLA_V)), MLA_KV_RANK ** -0.5),
        "mla_q_head_gain": 1.0 + nrm((n_even, MLA_QK), 0.02),
        "mla_k_head_gain": 1.0 + nrm((n_even, MLA_QK), 0.02),
        "w_out0": nrm((n_even, MIX_WIDTH, D_MODEL), MIX_WIDTH ** -0.5),
        "rwkv_mu": uni((n_odd, 6, D_MODEL), 0.0, 1.0),
        "rwkv_w_r": nrm((n_odd, D_MODEL, D_MODEL), D_MODEL ** -0.5),
        "rwkv_w_k": nrm((n_odd, D_MODEL, D_MODEL), D_MODEL ** -0.5),
        "rwkv_w_v": nrm((n_odd, D_MODEL, D_MODEL), D_MODEL ** -0.5),
        "rwkv_w_o": nrm((n_odd, D_MODEL, D_MODEL), D_MODEL ** -0.5),
        "rwkv_w0": uni((n_odd, D_MODEL), -6.0, -1.0),
        "rwkv_w1": nrm((n_odd, D_MODEL, DECAY_LORA), 0.1 * D_MODEL ** -0.5),
        "rwkv_w2": nrm((n_odd, DECAY_LORA, D_MODEL), 0.1 * DECAY_LORA ** -0.5),
        "rwkv_a0": nrm((n_odd, D_MODEL), 0.1),
        "rwkv_a1": nrm((n_odd, D_MODEL, AAA_LORA), 0.1 * D_MODEL ** -0.5),
        "rwkv_a2": nrm((n_odd, AAA_LORA, D_MODEL), 0.1 * AAA_LORA ** -0.5),
        "rwkv_g1": nrm((n_odd, D_MODEL, GATE_LORA), D_MODEL ** -0.5),
        "rwkv_g2": nrm((n_odd, GATE_LORA, D_MODEL), GATE_LORA ** -0.5),
        "rwkv_k_k": 0.85 + nrm((n_odd, D_MODEL), 0.02),
        "rwkv_k_a": 1.0 + nrm((n_odd, D_MODEL), 0.02),
        "rwkv_r_k": nrm((n_odd, RWKV_HEADS, RWKV_HEAD_DIM), 0.1),
        "rwkv_ln_gain": 1.0 + nrm((n_odd, D_MODEL), 0.02),
        "rwkv_ln_bias": nrm((n_odd, D_MODEL), 0.01),
        "w_up": nrm((DEPTH, D_MODEL, D_FF), D_MODEL ** -0.5),
        "w_down": nrm((DEPTH, D_FF, D_MODEL), D_FF ** -0.5),
    }


def reference(x, c, positions, w_mod, b_mod, w_in0, gdn_conv_w, gdn_a_log, gdn_dt_bias, gdn_norm_gain,
              mla_q_norm_gain, mla_kv_norm_gain, mla_w_uq, mla_w_ukv, mla_q_head_gain, mla_k_head_gain, w_out0,
              rwkv_mu, rwkv_w_r, rwkv_w_k, rwkv_w_v, rwkv_w_o, rwkv_w0, rwkv_w1, rwkv_w2,
              rwkv_a0, rwkv_a1, rwkv_a2, rwkv_g1, rwkv_g2, rwkv_k_k, rwkv_k_a, rwkv_r_k,
              rwkv_ln_gain, rwkv_ln_bias, w_up, w_down):
    cond = jax.nn.silu(c)
    cos, sin = rope_tables(positions)
    for layer in range(DEPTH):
        mod = cond @ w_mod[layer] + b_mod[layer]
        shift1, scale1, gate1, shift2, scale2, gate2 = jnp.split(mod, 6, axis=-1)
        h = adaln(x, shift1, scale1)
        i = layer // 2
        if layer % 2 == 0:
            y = gdn_mla_mixer(h, cos, sin, w_in0[i], gdn_conv_w[i], gdn_a_log[i], gdn_dt_bias[i],
                              gdn_norm_gain[i], mla_q_norm_gain[i], mla_kv_norm_gain[i], mla_w_uq[i],
                              mla_w_ukv[i], mla_q_head_gain[i], mla_k_head_gain[i], w_out0[i])
        else:
            y = rwkv7_time_mix(h, rwkv_mu[i], rwkv_w_r[i], rwkv_w_k[i], rwkv_w_v[i], rwkv_w_o[i],
                               rwkv_w0[i], rwkv_w1[i], rwkv_w2[i], rwkv_a0[i], rwkv_a1[i], rwkv_a2[i],
                               rwkv_g1[i], rwkv_g2[i], rwkv_k_k[i], rwkv_k_a[i], rwkv_r_k[i],
                               rwkv_ln_gain[i], rwkv_ln_bias[i])
        x = x + gate1[:, None, :] * y
        h = adaln(x, shift2, scale2)
        x = x + gate2[:, None, :] * squared_relu_mlp(h, w_up[layer], w_down[layer])
    return x
```

```python
import functools
import math

import jax
import jax.numpy as jnp
from jax import lax
from jax.experimental import pallas as pl
from jax.experimental.pallas import tpu as pltpu

F32 = jnp.float32
BF16 = jnp.bfloat16

D_MODEL = 1024
DEPTH = 2
GDN_HEADS = 8
GDN_HEAD_DIM = 64
GDN_WIDTH = GDN_HEADS * GDN_HEAD_DIM
GDN_CONV = 4
CHUNK = 64
MLA_HEADS = 8
MLA_Q_RANK = 256
MLA_KV_RANK = 128
MLA_NOPE = 64
MLA_ROPE = 32
MLA_QK = MLA_NOPE + MLA_ROPE
MLA_V = 64
ROPE_THETA = 10000.0
NEG_INF = -1e30
RWKV_HEADS = 16
RWKV_HEAD_DIM = 64
RWKV_GN_EPS = 64e-5
D_FF = 4 * D_MODEL
EPS = 1e-6
LANES = 128
SUBLANES = 8
VMEM_LIMIT = 56 * 1024 * 1024


def _params(*sem):
    return pltpu.CompilerParams(dimension_semantics=sem, vmem_limit_bytes=VMEM_LIMIT)


def _mm(a, b):
    return jnp.dot(a.astype(BF16), b.astype(BF16), preferred_element_type=F32)


def _mm_nt(a, b):
    return lax.dot_general(a.astype(BF16), b.astype(BF16), (((1,), (1,)), ((), ())),
                           preferred_element_type=F32)


def _mm_tn(a, b):
    return lax.dot_general(a.astype(BF16), b.astype(BF16), (((0,), (0,)), ((), ())),
                           preferred_element_type=F32)


def _tri_cumsum(tri_bf16, x):
    hi = x.astype(BF16)
    r1 = x - hi.astype(F32)
    mid = r1.astype(BF16)
    lo = (r1 - mid.astype(F32)).astype(BF16)
    dot = functools.partial(jnp.dot, preferred_element_type=F32)
    return dot(tri_bf16, hi) + dot(tri_bf16, mid) + dot(tri_bf16, lo)


def _unit_lower_inv(L, eye):
    n = L.shape[0]
    row = lax.broadcasted_iota(jnp.int32, (n, n), 0)
    col = lax.broadcasted_iota(jnp.int32, (n, n), 1)
    T = eye
    s = 1
    while s < n:
        lower_left = ((row // (2 * s)) == (col // (2 * s))) & ((row // s) % 2 == 1) & ((col // s) % 2 == 0)
        Cm = jnp.where(lower_left, L, 0.0)
        T = T - (Cm if s == 1 else _mm(_mm(T, Cm), T))
        s *= 2
    return T


def _softplus(x):
    return jnp.maximum(x, 0.0) + jnp.log1p(jnp.exp(-jnp.abs(x)))


def _silu(x):
    return x * jax.nn.sigmoid(x)


def _adaln(x, shift, scale):
    ms = jnp.mean(x * x, axis=-1, keepdims=True)
    return x * lax.rsqrt(ms + EPS) * (1.0 + scale) + shift


def _tri_masks(n):
    row = lax.broadcasted_iota(jnp.int32, (n, n), 0)
    col = lax.broadcasted_iota(jnp.int32, (n, n), 1)
    return row >= col, row > col, (row == col).astype(F32)


def _mod_kernel(c_ref, w_ref, b_ref, o_ref):
    cond = _silu(c_ref[...])
    o_ref[0] = jnp.dot(cond, w_ref[0], precision=lax.Precision.HIGHEST,
                       preferred_element_type=F32) + b_ref[0]


def _mod_call(c, w_mod, b_mod):
    B, D = c.shape
    depth, _, n = w_mod.shape
    tn = 1536
    c_pad = jnp.pad(c, ((0, SUBLANES - B), (0, 0)))
    out = pl.pallas_call(
        _mod_kernel,
        grid=(depth, n // tn),
        in_specs=[pl.BlockSpec((SUBLANES, D), lambda l, j: (0, 0)),
                  pl.BlockSpec((1, D, tn), lambda l, j: (l, 0, j)),
                  pl.BlockSpec((1, 1, tn), lambda l, j: (l, 0, j))],
        out_specs=pl.BlockSpec((1, SUBLANES, tn), lambda l, j: (l, 0, j)),
        out_shape=jax.ShapeDtypeStruct((depth, SUBLANES, n), F32),
        compiler_params=_params("parallel", "parallel"),
        name="mod",
    )(c_pad, w_mod, b_mod.reshape(depth, 1, n))
    return out[:, :B, :]


IN_PACKED = 2560


def _pack_w_in(w_in):
    D = w_in.shape[0]
    o_a = 4 * GDN_WIDTH
    o_dq = o_a + 2 * GDN_HEADS
    o_dkv = o_dq + MLA_Q_RANK
    o_kr = o_dkv + MLA_KV_RANK
    z = lambda n: jnp.zeros((D, n), w_in.dtype)
    return jnp.concatenate(
        [w_in[:, :o_a], w_in[:, o_dq:o_dkv], w_in[:, o_dkv:o_kr],
         w_in[:, o_a:o_dq], z(MLA_NOPE - 2 * GDN_HEADS), w_in[:, o_kr:o_kr + MLA_ROPE],
         z(LANES - MLA_NOPE - MLA_ROPE)], axis=1).astype(BF16)


def _inproj_kernel(x_ref, sh_ref, sc_ref, w_ref, qkvz_ref, lat_ref):
    h = _adaln(x_ref[0], sh_ref[0], sc_ref[0]).astype(BF16)
    y = jnp.dot(h, w_ref[...], preferred_element_type=F32)
    qkvz_ref[0] = y[:, :4 * GDN_WIDTH]
    lat_ref[0] = y[:, 4 * GDN_WIDTH:]


def _inproj_call(x, shift, scale, w_packed):
    B, S, D = x.shape
    tm = min(512, S)
    n_lat = IN_PACKED - 4 * GDN_WIDTH
    return pl.pallas_call(
        _inproj_kernel,
        grid=(B, S // tm),
        in_specs=[pl.BlockSpec((1, tm, D), lambda b, i: (b, i, 0)),
                  pl.BlockSpec((1, 1, D), lambda b, i: (b, 0, 0)),
                  pl.BlockSpec((1, 1, D), lambda b, i: (b, 0, 0)),
                  pl.BlockSpec((D, IN_PACKED), lambda b, i: (0, 0))],
        out_specs=[pl.BlockSpec((1, tm, 4 * GDN_WIDTH), lambda b, i: (b, i, 0)),
                   pl.BlockSpec((1, tm, n_lat), lambda b, i: (b, i, 0))],
        out_shape=[jax.ShapeDtypeStruct((B, S, 4 * GDN_WIDTH), F32),
                   jax.ShapeDtypeStruct((B, S, n_lat), F32)],
        compiler_params=_params("parallel", "parallel"),
        name="inproj",
    )(x, shift, scale, w_packed)


def _gdn_kernel(x_ref, ab_ref, cw_ref, alog_ref, dtb_ref, gain_ref, o_ref, ext_ref, st_ref):
    C, H, Dh, W = CHUNK, GDN_HEADS, GDN_HEAD_DIM, GDN_WIDTH

    @pl.when(pl.program_id(1) == 0)
    def _init():
        ext_ref[0:SUBLANES, :] = jnp.zeros((SUBLANES, 3 * W), F32)
        st_ref[...] = jnp.zeros_like(st_ref)

    x = x_ref[0]
    ext_ref[SUBLANES:SUBLANES + C, :] = x[:, :3 * W]
    base = SUBLANES - (GDN_CONV - 1)
    conv = cw_ref[0:1, :] * ext_ref[base:base + C, :]
    for j in range(1, GDN_CONV):
        conv = conv + cw_ref[j:j + 1, :] * ext_ref[base + j:base + j + C, :]
    ext_ref[0:SUBLANES, :] = x[C - SUBLANES:C, :3 * W]
    qkv = _silu(conv)

    causal, strict, eye = _tri_masks(C)
    tri = causal.astype(BF16)
    ab = ab_ref[0]
    beta_all = jax.nn.sigmoid(ab)
    g_all = -jnp.exp(alog_ref[...]) * _softplus(ab + dtb_ref[...])
    gcum = _tri_cumsum(tri, g_all)
    gcum_t = gcum.T
    eg = jnp.exp(gcum)
    glast = gcum[C - 1:C, :]
    eg_rest = jnp.exp(glast - gcum)
    eg_last = jnp.exp(glast)
    gain = gain_ref[...]

    for h in range(H):
        sl = slice(h * Dh, (h + 1) * Dh)
        q = qkv[:, sl]
        k = qkv[:, W + h * Dh:W + (h + 1) * Dh]
        v = qkv[:, 2 * W + h * Dh:2 * W + (h + 1) * Dh]
        z = x[:, 3 * W + h * Dh:3 * W + (h + 1) * Dh]
        q = q * lax.rsqrt(jnp.sum(q * q, axis=-1, keepdims=True) + 1e-12) * (Dh ** -0.5)
        k = k * lax.rsqrt(jnp.sum(k * k, axis=-1, keepdims=True) + 1e-12)
        bc = beta_all[:, H + h:H + h + 1]
        gdiff = gcum[:, h:h + 1] - gcum_t[h:h + 1, :]
        decay = jnp.where(causal, jnp.exp(jnp.where(causal, gdiff, 0.0)), 0.0)
        kb = k * bc
        L = jnp.where(strict, _mm_nt(kb, k) * decay, 0.0)
        T = _unit_lower_inv(L, eye)
        u = _mm(T, v * bc)
        w = _mm(T, kb * eg[:, h:h + 1])
        qk = jnp.where(causal, _mm_nt(q, k) * decay, 0.0)
        S = st_ref[h]
        v_new = u - _mm(w, S)
        o = _mm(q * eg[:, h:h + 1], S) + _mm(qk, v_new)
        st_ref[h] = S * eg_last[:, h:h + 1] + _mm_tn(k * eg_rest[:, h:h + 1], v_new)
        o = o * lax.rsqrt(jnp.mean(o * o, axis=-1, keepdims=True) + EPS) * gain * _silu(z)
        o_ref[0, :, sl] = o


def _gdn_call(qkvz, lat, conv_w, a_log, dt_bias, gain):
    B, S, _ = qkvz.shape
    C, H, Dh, W = CHUNK, GDN_HEADS, GDN_HEAD_DIM, GDN_WIDTH
    lane_pad = lambda t: jnp.pad(t.reshape(1, H), ((0, 0), (0, LANES - H)))
    ab_block = (lat.shape[-1] // LANES) - 1
    return pl.pallas_call(
        _gdn_kernel,
        grid=(B, S // C),
        in_specs=[pl.BlockSpec((1, C, 4 * W), lambda b, c: (b, c, 0)),
                  pl.BlockSpec((1, C, LANES), lambda b, c: (b, c, ab_block)),
                  pl.BlockSpec((GDN_CONV, 3 * W), lambda b, c: (0, 0)),
                  pl.BlockSpec((1, LANES), lambda b, c: (0, 0)),
                  pl.BlockSpec((1, LANES), lambda b, c: (0, 0)),
                  pl.BlockSpec((1, Dh), lambda b, c: (0, 0))],
        out_specs=pl.BlockSpec((1, C, W), lambda b, c: (b, c, 0)),
        out_shape=jax.ShapeDtypeStruct((B, S, W), F32),
        scratch_shapes=[pltpu.VMEM((C + SUBLANES, 3 * W), F32),
                        pltpu.VMEM((H, Dh, Dh), F32)],
        compiler_params=_params("parallel", "arbitrary"),
        name="gdn",
    )(qkvz, lat, conv_w, lane_pad(a_log), lane_pad(dt_bias), gain.reshape(1, Dh))


def _mla_prep_kernel(lat_ref, pos_ref, invf_ref, qg_ref, kvg_ref, wuq_ref, wuk_ref, wuv_ref,
                     qhg_ref, khg_ref, q_ref, k_ref, v_ref):
    lat = lat_ref[0]
    dq = lat[:, :MLA_Q_RANK]
    dkv = lat[:, MLA_Q_RANK:MLA_Q_RANK + MLA_KV_RANK]
    slab = lat[:, MLA_Q_RANK + MLA_KV_RANK:]
    q_lat = dq * lax.rsqrt(jnp.mean(dq * dq, axis=-1, keepdims=True) + EPS) * qg_ref[...]
    kv_lat = dkv * lax.rsqrt(jnp.mean(dkv * dkv, axis=-1, keepdims=True) + EPS) * kvg_ref[...]
    qm = _mm(q_lat, wuq_ref[...])
    kn = _mm(kv_lat, wuk_ref[...])
    v_ref[0] = _mm(kv_lat, wuv_ref[...]).astype(BF16)

    lane = lax.broadcasted_iota(jnp.int32, slab.shape, 1)
    half = MLA_ROPE // 2
    ang = pos_ref[0] * invf_ref[...]
    cosf = jnp.cos(ang)
    sinf = jnp.sin(ang)
    sin_signed = jnp.where(lane < MLA_NOPE + half, -sinf, sinf)
    k_rope = jnp.where((lane >= MLA_NOPE) & (lane < MLA_QK), slab, 0.0)

    def head_norm_rope(t, gain):
        t = t * lax.rsqrt(jnp.sum(t * t, axis=-1, keepdims=True) * (1.0 / MLA_QK) + EPS) * gain
        partner = jnp.where(lane < MLA_NOPE + half,
                            pltpu.roll(t, LANES - half, 1), pltpu.roll(t, half, 1))
        return t * cosf + partner * sin_signed

    scale = MLA_QK ** -0.5
    for h in range(MLA_HEADS):
        sl = slice(h * LANES, (h + 1) * LANES)
        q_ref[0, :, sl] = (head_norm_rope(qm[:, sl], qhg_ref[...]) * scale).astype(BF16)
        k_ref[0, :, sl] = head_norm_rope(kn[:, sl] + k_rope, khg_ref[...]).astype(BF16)


def _mla_prep_call(lat, positions, q_norm_gain, kv_norm_gain, w_uq, w_ukv, q_head_gain, k_head_gain):
    B, S, n_lat = lat.shape
    H = MLA_HEADS
    tm = min(512, S)
    wuq = jnp.pad(w_uq.reshape(MLA_Q_RANK, H, MLA_QK), ((0, 0), (0, 0), (0, LANES - MLA_QK)))
    wuq = wuq.reshape(MLA_Q_RANK, H * LANES).astype(BF16)
    wkv = w_ukv.reshape(MLA_KV_RANK, H, MLA_NOPE + MLA_V)
    wuk = jnp.pad(wkv[:, :, :MLA_NOPE], ((0, 0), (0, 0), (0, LANES - MLA_NOPE)))
    wuk = wuk.reshape(MLA_KV_RANK, H * LANES).astype(BF16)
    wuv = wkv[:, :, MLA_NOPE:].reshape(MLA_KV_RANK, H * MLA_V).astype(BF16)
    pad_gain = lambda g: jnp.pad(g.reshape(1, MLA_QK), ((0, 0), (0, LANES - MLA_QK)))
    inv_freq = ROPE_THETA ** (-jnp.arange(0, MLA_ROPE, 2, dtype=F32) / MLA_ROPE)
    invf = jnp.concatenate([jnp.zeros((MLA_NOPE,), F32), inv_freq, inv_freq,
                            jnp.zeros((LANES - MLA_QK,), F32)]).reshape(1, LANES)
    pos = positions.astype(F32).reshape(B, S, 1)
    const = lambda shape: pl.BlockSpec(shape, lambda b, i: (0,) * len(shape))
    return pl.pallas_call(
        _mla_prep_kernel,
        grid=(B, S // tm),
        in_specs=[pl.BlockSpec((1, tm, n_lat), lambda b, i: (b, i, 0)),
                  pl.BlockSpec((1, tm, 1), lambda b, i: (b, i, 0)),
                  const((1, LANES)), const((1, MLA_Q_RANK)), const((1, MLA_KV_RANK)),
                  const((MLA_Q_RANK, H * LANES)), const((MLA_KV_RANK, H * LANES)),
                  const((MLA_KV_RANK, H * MLA_V)), const((1, LANES)), const((1, LANES))],
        out_specs=[pl.BlockSpec((1, tm, H * LANES), lambda b, i: (b, i, 0)),
                   pl.BlockSpec((1, tm, H * LANES), lambda b, i: (b, i, 0)),
                   pl.BlockSpec((1, tm, H * MLA_V), lambda b, i: (b, i, 0))],
        out_shape=[jax.ShapeDtypeStruct((B, S, H * LANES), BF16),
                   jax.ShapeDtypeStruct((B, S, H * LANES), BF16),
                   jax.ShapeDtypeStruct((B, S, H * MLA_V), BF16)],
        compiler_params=_params("parallel", "parallel"),
        name="mla_prep",
    )(lat, pos, invf, q_norm_gain.reshape(1, -1), kv_norm_gain.reshape(1, -1), wuq, wuk, wuv,
      pad_gain(q_head_gain), pad_gain(k_head_gain))


def _flash_kernel(q_ref, k_ref, v_ref, o_ref, *, tq):
    qi = pl.program_id(2)
    q = q_ref[0]
    q0, q1 = q[:, :LANES], q[:, LANES:]
    lane = lax.broadcasted_iota(jnp.int32, (tq, LANES), 1)
    first = lane < MLA_V
    row = lax.broadcasted_iota(jnp.int32, (tq, tq), 0)
    col = lax.broadcasted_iota(jnp.int32, (tq, tq), 1)
    diag_mask = row >= col

    def online(s, m, l):
        m_new = jnp.maximum(m, jnp.max(s, axis=-1, keepdims=True))
        p = jnp.exp(s - m_new)
        alpha = jnp.exp(m - m_new)
        return m_new, alpha * l + jnp.sum(p, axis=-1, keepdims=True), alpha, p

    def block(kj, carry, masked):
        m0, l0, m1, l1, acc = carry
        start = pl.multiple_of(kj * tq, tq)
        kb = k_ref[0, pl.ds(start, tq), :]
        vb = v_ref[0, pl.ds(start, tq), :]
        s0 = _mm_nt(q0, kb[:, :LANES])
        s1 = _mm_nt(q1, kb[:, LANES:])
        if masked:
            s0 = jnp.where(diag_mask, s0, NEG_INF)
            s1 = jnp.where(diag_mask, s1, NEG_INF)
        m0, l0, a0, p0 = online(s0, m0, l0)
        m1, l1, a1, p1 = online(s1, m1, l1)
        pv = jnp.where(first, _mm(p0, vb), _mm(p1, vb))
        acc = jnp.where(first, a0, a1) * acc + pv
        return m0, l0, m1, l1, acc

    neg = jnp.full((tq, 1), NEG_INF, F32)
    zero = jnp.zeros((tq, 1), F32)
    carry = (neg, zero, neg, zero, jnp.zeros((tq, LANES), F32))
    carry = lax.fori_loop(0, qi, lambda kj, c: block(kj, c, False), carry)
    m0, l0, m1, l1, acc = block(qi, carry, True)
    o_ref[0] = (acc * jnp.where(first, 1.0 / l0, 1.0 / l1)).astype(o_ref.dtype)


def _flash_call(q, k, v):
    B, S, _ = q.shape
    tq = min(512, S)
    pairs = MLA_HEADS // 2
    return pl.pallas_call(
        functools.partial(_flash_kernel, tq=tq),
        grid=(B, pairs, S // tq),
        in_specs=[pl.BlockSpec((1, tq, 2 * LANES), lambda b, p, i: (b, i, p)),
                  pl.BlockSpec((1, S, 2 * LANES), lambda b, p, i: (b, 0, p)),
                  pl.BlockSpec((1, S, 2 * MLA_V), lambda b, p, i: (b, 0, p))],
        out_specs=pl.BlockSpec((1, tq, 2 * MLA_V), lambda b, p, i: (b, i, p)),
        out_shape=jax.ShapeDtypeStruct((B, S, MLA_HEADS * MLA_V), BF16),
        compiler_params=_params("parallel", "parallel", "arbitrary"),
        name="flash",
    )(q, k, v)


def _outproj_kernel(*refs, n_in):
    ins, (x_ref, gate_ref, w_ref, o_ref) = refs[:n_in], refs[n_in:]
    y = None
    off = 0
    for r in ins:
        n = r.shape[-1]
        part = _mm(r[0], w_ref[off:off + n, :])
        y = part if y is None else y + part
        off += n
    o_ref[0] = x_ref[0] + gate_ref[0] * y


def _outproj_call(ins, x, gate, w):
    B, S, D = x.shape
    tm = min(512, S)
    n_in = len(ins)
    return pl.pallas_call(
        functools.partial(_outproj_kernel, n_in=n_in),
        grid=(B, S // tm),
        in_specs=[pl.BlockSpec((1, tm, a.shape[-1]), lambda b, i: (b, i, 0)) for a in ins]
        + [pl.BlockSpec((1, tm, D), lambda b, i: (b, i, 0)),
           pl.BlockSpec((1, 1, D), lambda b, i: (b, 0, 0)),
           pl.BlockSpec(w.shape, lambda b, i: (0, 0))],
        out_specs=pl.BlockSpec((1, tm, D), lambda b, i: (b, i, 0)),
        out_shape=jax.ShapeDtypeStruct((B, S, D), F32),
        compiler_params=_params("parallel", "parallel"),
        name="outproj",
    )(*ins, x, gate, w)


def _mlp_kernel(x_ref, sh_ref, sc_ref, gate_ref, wu_ref, wd_ref, o_ref, h_ref, acc_ref):
    f = pl.program_id(2)

    @pl.when(f == 0)
    def _first():
        h_ref[...] = _adaln(x_ref[0], sh_ref[0], sc_ref[0]).astype(BF16)
        acc_ref[...] = jnp.zeros_like(acc_ref)

    u = jnp.maximum(jnp.dot(h_ref[...], wu_ref[...], preferred_element_type=F32), 0.0)
    acc_ref[...] += _mm(u * u, wd_ref[...])

    @pl.when(f == pl.num_programs(2) - 1)
    def _last():
        o_ref[0] = x_ref[0] + gate_ref[0] * acc_ref[...]


def _mlp_call(x, shift, scale, gate, w_up, w_down):
    B, S, D = x.shape
    F = w_up.shape[1]
    tm = min(1024, S)
    tf = 512
    vec = pl.BlockSpec((1, 1, D), lambda b, i, f: (b, 0, 0))
    return pl.pallas_call(
        _mlp_kernel,
        grid=(B, S // tm, F // tf),
        in_specs=[pl.BlockSpec((1, tm, D), lambda b, i, f: (b, i, 0)), vec, vec, vec,
                  pl.BlockSpec((D, tf), lambda b, i, f: (0, f)),
                  pl.BlockSpec((tf, D), lambda b, i, f: (f, 0))],
        out_specs=pl.BlockSpec((1, tm, D), lambda b, i, f: (b, i, 0)),
        out_shape=jax.ShapeDtypeStruct((B, S, D), F32),
        scratch_shapes=[pltpu.VMEM((tm, D), BF16), pltpu.VMEM((tm, D), F32)],
        compiler_params=_params("parallel", "parallel", "arbitrary"),
        name="mlp",
    )(x, shift, scale, gate, w_up, w_down)


def _rwkv_proj_kernel(x_ref, xp_ref, sh_ref, sc_ref, mu_ref, wr_ref, wk_ref, wv_ref,
                      w1_ref, w2_ref, a1_ref, a2_ref, g1_ref, g2_ref, w0_ref, a0_ref,
                      r_ref, w_ref, k_ref, v_ref, a_ref, g_ref):
    h = _adaln(x_ref[0], sh_ref[0], sc_ref[0])
    tm = h.shape[0]
    prev_tail = _adaln(xp_ref[0], sh_ref[0], sc_ref[0])[SUBLANES - 1:SUBLANES, :]
    prev_tail = jnp.where(pl.program_id(1) == 0, 0.0, prev_tail)
    row = lax.broadcasted_iota(jnp.int32, h.shape, 0)
    h_prev = jnp.where(row == 0, prev_tail, pltpu.roll(h, 1, 0))
    xx = h_prev - h
    mix = lambda j: h + xx * mu_ref[j:j + 1, :]
    xr, xw, xk, xv, xa, xg = (mix(j) for j in range(6))
    r_ref[0] = _mm(xr, wr_ref[...])
    k_ref[0] = _mm(xk, wk_ref[...])
    v_ref[0] = _mm(xv, wv_ref[...])
    wl = w0_ref[...] + _mm(jnp.tanh(_mm(xw, w1_ref[...])), w2_ref[...])
    w_log = -_softplus(-wl) - 0.5
    w_ref[0] = -jnp.exp(w_log)
    a_ref[0] = jax.nn.sigmoid(a0_ref[...] + _mm(_mm(xa, a1_ref[...]), a2_ref[...]))
    g_ref[0] = _mm(jax.nn.sigmoid(_mm(xg, g1_ref[...])), g2_ref[...])


def _pad_lora(w_a, w_b):
    r = w_a.shape[1]
    rp = -(-r // LANES) * LANES
    return (jnp.pad(w_a, ((0, 0), (0, rp - r))).astype(BF16),
            jnp.pad(w_b, ((0, rp - r), (0, 0))).astype(BF16))


def _rwkv_proj_call(x, shift, scale, mu, w_r, w_k, w_v, w1, w2, a1, a2, g1, g2, w0, a0):
    B, S, D = x.shape
    tm = min(256, S)
    w1p, w2p = _pad_lora(w1, w2)
    a1p, a2p = _pad_lora(a1, a2)
    g1p, g2p = _pad_lora(g1, g2)
    const = lambda t: pl.BlockSpec(t.shape, lambda b, i: (0,) * t.ndim)
    vec = pl.BlockSpec((1, 1, D), lambda b, i: (b, 0, 0))
    row_blocks = tm // SUBLANES
    weights = (mu, w_r.astype(BF16), w_k.astype(BF16), w_v.astype(BF16), w1p, w2p, a1p, a2p,
               g1p, g2p, w0.reshape(1, D), a0.reshape(1, D))
    out = pl.BlockSpec((1, tm, D), lambda b, i: (b, i, 0))
    return pl.pallas_call(
        _rwkv_proj_kernel,
        grid=(B, S // tm),
        in_specs=[pl.BlockSpec((1, tm, D), lambda b, i: (b, i, 0)),
                  pl.BlockSpec((1, SUBLANES, D),
                               lambda b, i: (b, jnp.maximum(i * row_blocks - 1, 0), 0)),
                  vec, vec] + [const(t) for t in weights],
        out_specs=[out] * 6,
        out_shape=[jax.ShapeDtypeStruct((B, S, D), F32)] * 6,
        compiler_params=_params("parallel", "parallel"),
        name="rwkv_proj",
    )(x, x, shift, scale, *weights)


def _rwkv_kernel(r_ref, w_ref, k_ref, v_ref, a_ref, g_ref, kk_ref, ka_ref, rk_ref, lng_ref, lnb_ref,
                 o_ref, st_ref):
    C, H, Dh = CHUNK, RWKV_HEADS, RWKV_HEAD_DIM

    @pl.when(pl.program_id(1) == 0)
    def _init():
        st_ref[...] = jnp.zeros_like(st_ref)

    incl, strict, eye = _tri_masks(C)
    w = w_ref[0]
    G = _tri_cumsum(incl.astype(BF16), w)
    g_last = G[C - 1:C, :]
    e_in = jnp.exp(G)
    e_ex = jnp.exp(G - w)
    e_neg = jnp.exp(-G)
    e_rest = jnp.exp(g_last - G)
    e_last = jnp.exp(g_last)
    r = r_ref[0]
    k0 = k_ref[0]
    v = v_ref[0]
    a = a_ref[0]
    g = g_ref[0]
    kk_raw = k0 * kk_ref[...]
    k_mod = k0 * (1.0 + (a - 1.0) * ka_ref[...])
    rk = rk_ref[...]
    lng = lng_ref[...]
    lnb = lnb_ref[...]

    for h in range(H):
        sl = slice(h * Dh, (h + 1) * Dh)
        kk = kk_raw[:, sl]
        kk = kk * lax.rsqrt(jnp.sum(kk * kk, axis=-1, keepdims=True) + 1e-12)
        kh, ah, rh, vh = k_mod[:, sl], a[:, sl], r[:, sl], v[:, sl]
        bv = kk * ah
        a_t = -kk * e_ex[:, sl]
        b_t = bv * e_neg[:, sl]
        k_t = kh * e_neg[:, sl]
        r_t = rh * e_in[:, sl]
        a_ab = jnp.where(strict, _mm_nt(a_t, b_t), 0.0)
        a_ak = jnp.where(strict, _mm_nt(a_t, k_t), 0.0)
        a_rb = jnp.where(incl, _mm_nt(r_t, b_t), 0.0)
        a_rk = jnp.where(incl, _mm_nt(r_t, k_t), 0.0)
        T = _unit_lower_inv(-a_ab, eye)
        S = st_ref[h]
        P = _mm(T, _mm_nt(a_t, S) + _mm(a_ak, vh))
        y = _mm_nt(r_t, S) + _mm(a_rb, P) + _mm(a_rk, vh)
        st_ref[h] = (S * e_last[:, sl] + _mm_tn(P, bv * e_rest[:, sl])
                     + _mm_tn(vh, kh * e_rest[:, sl]))
        mean = jnp.mean(y, axis=-1, keepdims=True)
        yc = y - mean
        var = jnp.mean(yc * yc, axis=-1, keepdims=True)
        yn = yc * lax.rsqrt(var + RWKV_GN_EPS) * lng[:, sl] + lnb[:, sl]
        bonus = jnp.sum(rh * kh * rk[:, sl], axis=-1, keepdims=True) * vh
        o_ref[0, :, sl] = (yn + bonus) * g[:, sl]


def _rwkv_call(r, w, k, v, a, g, k_k, k_a, r_k, ln_gain, ln_bias):
    B, S, D = r.shape
    C, H, Dh = CHUNK, RWKV_HEADS, RWKV_HEAD_DIM
    seq = pl.BlockSpec((1, C, D), lambda b, c: (b, c, 0))
    vec = pl.BlockSpec((1, D), lambda b, c: (0, 0))
    flat = lambda t: t.reshape(1, D)
    return pl.pallas_call(
        _rwkv_kernel,
        grid=(B, S // C),
        in_specs=[seq] * 6 + [vec] * 5,
        out_specs=seq,
        out_shape=jax.ShapeDtypeStruct((B, S, D), F32),
        scratch_shapes=[pltpu.VMEM((H, Dh, Dh), F32)],
        compiler_params=_params("parallel", "arbitrary"),
        name="rwkv",
    )(r, w, k, v, a, g, flat(k_k), flat(k_a), flat(r_k), flat(ln_gain), flat(ln_bias))


def kernel(x, c, positions, w_mod, b_mod, w_in0, gdn_conv_w, gdn_a_log, gdn_dt_bias, gdn_norm_gain, mla_q_norm_gain, mla_kv_norm_gain, mla_w_uq, mla_w_ukv, mla_q_head_gain, mla_k_head_gain, w_out0, rwkv_mu, rwkv_w_r, rwkv_w_k, rwkv_w_v, rwkv_w_o, rwkv_w0, rwkv_w1, rwkv_w2, rwkv_a0, rwkv_a1, rwkv_a2, rwkv_g1, rwkv_g2, rwkv_k_k, rwkv_k_a, rwkv_r_k, rwkv_ln_gain, rwkv_ln_bias, w_up, w_down):
    B, S, D = x.shape
    mod = _mod_call(c, w_mod, b_mod)

    def mod_chunks(layer):
        return [mod[layer, :, i * D:(i + 1) * D].reshape(B, 1, D) for i in range(6)]

    shift1, scale1, gate1, shift2, scale2, gate2 = mod_chunks(0)
    qkvz, lat = _inproj_call(x, shift1, scale1, _pack_w_in(w_in0[0]))
    o_gdn = _gdn_call(qkvz, lat, gdn_conv_w[0], gdn_a_log[0], gdn_dt_bias[0], gdn_norm_gain[0])
    q, k, v = _mla_prep_call(lat, positions, mla_q_norm_gain[0], mla_kv_norm_gain[0], mla_w_uq[0],
                             mla_w_ukv[0], mla_q_head_gain[0], mla_k_head_gain[0])
    o_mla = _flash_call(q, k, v)
    x = _outproj_call([o_gdn, o_mla], x, gate1, w_out0[0].astype(BF16))
    x = _mlp_call(x, shift2, scale2, gate2, w_up[0].astype(BF16), w_down[0].astype(BF16))

    shift1, scale1, gate1, shift2, scale2, gate2 = mod_chunks(1)
    r, w, k, v, a, g = _rwkv_proj_call(x, shift1, scale1, rwkv_mu[0], rwkv_w_r[0], rwkv_w_k[0],
                                       rwkv_w_v[0], rwkv_w1[0], rwkv_w2[0], rwkv_a1[0], rwkv_a2[0],
                                       rwkv_g1[0], rwkv_g2[0], rwkv_w0[0], rwkv_a0[0])
    y = _rwkv_call(r, w, k, v, a, g, rwkv_k_k[0], rwkv_k_a[0], rwkv_r_k[0], rwkv_ln_gain[0],
                   rwkv_ln_bias[0])
    x = _outproj_call([y], x, gate1, rwkv_w_o[0].astype(BF16))
    x = _mlp_call(x, shift2, scale2, gate2, w_up[1].astype(BF16), w_down[1].astype(BF16))
    return x
```

```python
import functools
import math

import jax
import jax.numpy as jnp
from jax import lax
from jax.experimental import pallas as pl
from jax.experimental.pallas import tpu as pltpu

F32 = jnp.float32
BF16 = jnp.bfloat16

D_MODEL = 1024
DEPTH = 2
GDN_HEADS = 8
GDN_HEAD_DIM = 64
GDN_WIDTH = GDN_HEADS * GDN_HEAD_DIM
GDN_CONV = 4
CHUNK = 64
MLA_HEADS = 8
MLA_Q_RANK = 256
MLA_KV_RANK = 128
MLA_NOPE = 64
MLA_ROPE = 32
MLA_QK = MLA_NOPE + MLA_ROPE
MLA_V = 64
ROPE_THETA = 10000.0
NEG_INF = -1e30
RWKV_HEADS = 16
RWKV_HEAD_DIM = 64
RWKV_GN_EPS = 64e-5
D_FF = 4 * D_MODEL
EPS = 1e-6
LANES = 128
SUBLANES = 8
VMEM_LIMIT = 56 * 1024 * 1024


def _params(*sem):
    return pltpu.CompilerParams(dimension_semantics=sem, vmem_limit_bytes=VMEM_LIMIT)


def _mm(a, b):
    return jnp.dot(a.astype(BF16), b.astype(BF16), preferred_element_type=F32)


def _mm_nt(a, b):
    return lax.dot_general(a.astype(BF16), b.astype(BF16), (((1,), (1,)), ((), ())),
                           preferred_element_type=F32)


def _mm_tn(a, b):
    return lax.dot_general(a.astype(BF16), b.astype(BF16), (((0,), (0,)), ((), ())),
                           preferred_element_type=F32)


def _tri_cumsum(tri_bf16, x):
    hi = x.astype(BF16)
    r1 = x - hi.astype(F32)
    mid = r1.astype(BF16)
    lo = (r1 - mid.astype(F32)).astype(BF16)
    dot = functools.partial(jnp.dot, preferred_element_type=F32)
    return dot(tri_bf16, hi) + dot(tri_bf16, mid) + dot(tri_bf16, lo)


def _unit_lower_inv(Ls, eye):
    n = Ls[0].shape[0]
    row = lax.broadcasted_iota(jnp.int32, (n, n), 0)
    col = lax.broadcasted_iota(jnp.int32, (n, n), 1)
    Ts = [eye] * len(Ls)
    s = 1
    while s < n:
        lower_left = ((row // (2 * s)) == (col // (2 * s))) & ((row // s) % 2 == 1) & ((col // s) % 2 == 0)
        Cs = [jnp.where(lower_left, L, 0.0) for L in Ls]
        if s == 1:
            Ts = [T - C for T, C in zip(Ts, Cs)]
        else:
            Xs = [_mm(T, C) for T, C in zip(Ts, Cs)]
            Ts = [T - _mm(X, T) for T, X in zip(Ts, Xs)]
        s *= 2
    return Ts


def _softplus(x):
    return jnp.maximum(x, 0.0) + jnp.log1p(jnp.exp(-jnp.abs(x)))


def _silu(x):
    return x * jax.nn.sigmoid(x)


def _adaln(x, shift, scale):
    ms = jnp.mean(x * x, axis=-1, keepdims=True)
    return x * lax.rsqrt(ms + EPS) * (1.0 + scale) + shift


def _tri_masks(n):
    row = lax.broadcasted_iota(jnp.int32, (n, n), 0)
    col = lax.broadcasted_iota(jnp.int32, (n, n), 1)
    return row >= col, row > col, (row == col).astype(F32)


def _mod_kernel(c_ref, w_ref, b_ref, o_ref):
    cond = _silu(c_ref[...])
    o_ref[0] = jnp.dot(cond, w_ref[0], precision=lax.Precision.HIGHEST,
                       preferred_element_type=F32) + b_ref[0]


def _mod_call(c, w_mod, b_mod):
    B, D = c.shape
    depth, _, n = w_mod.shape
    tn = 1536
    c_pad = jnp.pad(c, ((0, SUBLANES - B), (0, 0)))
    out = pl.pallas_call(
        _mod_kernel,
        grid=(depth, n // tn),
        in_specs=[pl.BlockSpec((SUBLANES, D), lambda l, j: (0, 0)),
                  pl.BlockSpec((1, D, tn), lambda l, j: (l, 0, j)),
                  pl.BlockSpec((1, 1, tn), lambda l, j: (l, 0, j))],
        out_specs=pl.BlockSpec((1, SUBLANES, tn), lambda l, j: (l, 0, j)),
        out_shape=jax.ShapeDtypeStruct((depth, SUBLANES, n), F32),
        compiler_params=_params("parallel", "parallel"),
        name="mod",
    )(c_pad, w_mod, b_mod.reshape(depth, 1, n))
    return out[:, :B, :]


IN_PACKED = 2560


def _pack_w_in(w_in):
    D = w_in.shape[0]
    o_a = 4 * GDN_WIDTH
    o_dq = o_a + 2 * GDN_HEADS
    o_dkv = o_dq + MLA_Q_RANK
    o_kr = o_dkv + MLA_KV_RANK
    z = lambda n: jnp.zeros((D, n), w_in.dtype)
    return jnp.concatenate(
        [w_in[:, :o_a], w_in[:, o_dq:o_dkv], w_in[:, o_dkv:o_kr],
         w_in[:, o_a:o_dq], z(MLA_NOPE - 2 * GDN_HEADS), w_in[:, o_kr:o_kr + MLA_ROPE],
         z(LANES - MLA_NOPE - MLA_ROPE)], axis=1).astype(BF16)


def _inproj_kernel(x_ref, sh_ref, sc_ref, w_ref, qkvz_ref, lat_ref):
    h = _adaln(x_ref[0], sh_ref[0], sc_ref[0]).astype(BF16)
    y = jnp.dot(h, w_ref[...], preferred_element_type=F32)
    qkvz_ref[0] = y[:, :4 * GDN_WIDTH]
    lat_ref[0] = y[:, 4 * GDN_WIDTH:]


def _inproj_call(x, shift, scale, w_packed):
    B, S, D = x.shape
    tm = min(512, S)
    n_lat = IN_PACKED - 4 * GDN_WIDTH
    return pl.pallas_call(
        _inproj_kernel,
        grid=(B, S // tm),
        in_specs=[pl.BlockSpec((1, tm, D), lambda b, i: (b, i, 0)),
                  pl.BlockSpec((1, 1, D), lambda b, i: (b, 0, 0)),
                  pl.BlockSpec((1, 1, D), lambda b, i: (b, 0, 0)),
                  pl.BlockSpec((D, IN_PACKED), lambda b, i: (0, 0))],
        out_specs=[pl.BlockSpec((1, tm, 4 * GDN_WIDTH), lambda b, i: (b, i, 0)),
                   pl.BlockSpec((1, tm, n_lat), lambda b, i: (b, i, 0))],
        out_shape=[jax.ShapeDtypeStruct((B, S, 4 * GDN_WIDTH), F32),
                   jax.ShapeDtypeStruct((B, S, n_lat), F32)],
        compiler_params=_params("parallel", "parallel"),
        name="inproj",
    )(x, shift, scale, w_packed)


def _gdn_kernel(x_ref, ab_ref, cw_ref, alog_ref, dtb_ref, gain_ref, o_ref, ext_ref, st_ref):
    C, H, Dh, W = CHUNK, GDN_HEADS, GDN_HEAD_DIM, GDN_WIDTH

    @pl.when(pl.program_id(1) == 0)
    def _init():
        ext_ref[0:SUBLANES, :] = jnp.zeros((SUBLANES, 3 * W), F32)
        st_ref[...] = jnp.zeros_like(st_ref)

    x = x_ref[0]
    ext_ref[SUBLANES:SUBLANES + C, :] = x[:, :3 * W]
    base = SUBLANES - (GDN_CONV - 1)
    conv = cw_ref[0:1, :] * ext_ref[base:base + C, :]
    for j in range(1, GDN_CONV):
        conv = conv + cw_ref[j:j + 1, :] * ext_ref[base + j:base + j + C, :]
    ext_ref[0:SUBLANES, :] = x[C - SUBLANES:C, :3 * W]
    qkv = _silu(conv)

    causal, strict, eye = _tri_masks(C)
    tri = causal.astype(BF16)
    ab = ab_ref[0]
    beta_all = jax.nn.sigmoid(ab)
    g_all = -jnp.exp(alog_ref[...]) * _softplus(ab + dtb_ref[...])
    gcum = _tri_cumsum(tri, g_all)
    gcum_t = gcum.T
    eg = jnp.exp(gcum)
    glast = gcum[C - 1:C, :]
    eg_rest = jnp.exp(glast - gcum)
    eg_last = jnp.exp(glast)
    gain = gain_ref[...]

    heads = range(H)
    head = lambda t, off, h: t[:, off + h * Dh:off + (h + 1) * Dh]
    l2n = lambda t: t * lax.rsqrt(jnp.sum(t * t, axis=-1, keepdims=True) + 1e-12)
    q = [l2n(head(qkv, 0, h)) * (Dh ** -0.5) for h in heads]
    k = [l2n(head(qkv, W, h)) for h in heads]
    v = [head(qkv, 2 * W, h) for h in heads]
    bc = [beta_all[:, H + h:H + h + 1] for h in heads]
    decay = [jnp.where(causal, jnp.exp(jnp.where(causal, gcum[:, h:h + 1] - gcum_t[h:h + 1, :], 0.0)), 0.0)
             for h in heads]
    kb = [k[h] * bc[h] for h in heads]
    S = [st_ref[h] for h in heads]
    L = [jnp.where(strict, _mm_nt(kb[h], k[h]) * decay[h], 0.0) for h in heads]
    qk = [jnp.where(causal, _mm_nt(q[h], k[h]) * decay[h], 0.0) for h in heads]
    rhs = [v[h] * bc[h] - _mm(kb[h] * eg[:, h:h + 1], S[h]) for h in heads]
    o_state = [_mm(q[h] * eg[:, h:h + 1], S[h]) for h in heads]
    T = _unit_lower_inv(L, eye)
    v_new = [_mm(T[h], rhs[h]) for h in heads]
    o = [o_state[h] + _mm(qk[h], v_new[h]) for h in heads]
    for h in heads:
        st_ref[h] = S[h] * eg_last[:, h:h + 1] + _mm_tn(k[h] * eg_rest[:, h:h + 1], v_new[h])
    for h in heads:
        oh = o[h] * lax.rsqrt(jnp.mean(o[h] * o[h], axis=-1, keepdims=True) + EPS)
        o_ref[0, :, h * Dh:(h + 1) * Dh] = oh * gain * _silu(head(x, 3 * W, h))


def _gdn_call(qkvz, lat, conv_w, a_log, dt_bias, gain):
    B, S, _ = qkvz.shape
    C, H, Dh, W = CHUNK, GDN_HEADS, GDN_HEAD_DIM, GDN_WIDTH
    lane_pad = lambda t: jnp.pad(t.reshape(1, H), ((0, 0), (0, LANES - H)))
    ab_block = (lat.shape[-1] // LANES) - 1
    return pl.pallas_call(
        _gdn_kernel,
        grid=(B, S // C),
        in_specs=[pl.BlockSpec((1, C, 4 * W), lambda b, c: (b, c, 0)),
                  pl.BlockSpec((1, C, LANES), lambda b, c: (b, c, ab_block)),
                  pl.BlockSpec((GDN_CONV, 3 * W), lambda b, c: (0, 0)),
                  pl.BlockSpec((1, LANES), lambda b, c: (0, 0)),
                  pl.BlockSpec((1, LANES), lambda b, c: (0, 0)),
                  pl.BlockSpec((1, Dh), lambda b, c: (0, 0))],
        out_specs=pl.BlockSpec((1, C, W), lambda b, c: (b, c, 0)),
        out_shape=jax.ShapeDtypeStruct((B, S, W), F32),
        scratch_shapes=[pltpu.VMEM((C + SUBLANES, 3 * W), F32),
                        pltpu.VMEM((H, Dh, Dh), F32)],
        compiler_params=_params("parallel", "arbitrary"),
        name="gdn",
    )(qkvz, lat, conv_w, lane_pad(a_log), lane_pad(dt_bias), gain.reshape(1, Dh))


def _mla_prep_kernel(lat_ref, pos_ref, invf_ref, qg_ref, kvg_ref, wuq_ref, wuk_ref, wuv_ref,
                     qhg_ref, khg_ref, q_ref, k_ref, v_ref):
    lat = lat_ref[0]
    dq = lat[:, :MLA_Q_RANK]
    dkv = lat[:, MLA_Q_RANK:MLA_Q_RANK + MLA_KV_RANK]
    slab = lat[:, MLA_Q_RANK + MLA_KV_RANK:]
    q_lat = dq * lax.rsqrt(jnp.mean(dq * dq, axis=-1, keepdims=True) + EPS) * qg_ref[...]
    kv_lat = dkv * lax.rsqrt(jnp.mean(dkv * dkv, axis=-1, keepdims=True) + EPS) * kvg_ref[...]
    qm = _mm(q_lat, wuq_ref[...])
    kn = _mm(kv_lat, wuk_ref[...])
    v_lane = lax.broadcasted_iota(jnp.int32, (1, MLA_HEADS * LANES), 1)
    v_ones = ((v_lane % LANES) >= MLA_V).astype(F32)
    v_ref[0] = (_mm(kv_lat, wuv_ref[...]) + v_ones).astype(BF16)

    lane = lax.broadcasted_iota(jnp.int32, slab.shape, 1)
    half = MLA_ROPE // 2
    ang = pos_ref[0] * invf_ref[...]
    cosf = jnp.cos(ang)
    sinf = jnp.sin(ang)
    sin_signed = jnp.where(lane < MLA_NOPE + half, -sinf, sinf)
    k_rope = jnp.where((lane >= MLA_NOPE) & (lane < MLA_QK), slab, 0.0)

    def head_norm_rope(t, gain):
        t = t * lax.rsqrt(jnp.sum(t * t, axis=-1, keepdims=True) * (1.0 / MLA_QK) + EPS) * gain
        partner = jnp.where(lane < MLA_NOPE + half,
                            pltpu.roll(t, LANES - half, 1), pltpu.roll(t, half, 1))
        return t * cosf + partner * sin_signed

    scale = MLA_QK ** -0.5
    for h in range(MLA_HEADS):
        sl = slice(h * LANES, (h + 1) * LANES)
        q_ref[0, :, sl] = (head_norm_rope(qm[:, sl], qhg_ref[...]) * scale).astype(BF16)
        k_ref[0, :, sl] = head_norm_rope(kn[:, sl] + k_rope, khg_ref[...]).astype(BF16)


def _mla_prep_call(lat, positions, q_norm_gain, kv_norm_gain, w_uq, w_ukv, q_head_gain, k_head_gain):
    B, S, n_lat = lat.shape
    H = MLA_HEADS
    tm = min(512, S)
    wuq = jnp.pad(w_uq.reshape(MLA_Q_RANK, H, MLA_QK), ((0, 0), (0, 0), (0, LANES - MLA_QK)))
    wuq = wuq.reshape(MLA_Q_RANK, H * LANES).astype(BF16)
    wkv = w_ukv.reshape(MLA_KV_RANK, H, MLA_NOPE + MLA_V)
    wuk = jnp.pad(wkv[:, :, :MLA_NOPE], ((0, 0), (0, 0), (0, LANES - MLA_NOPE)))
    wuk = wuk.reshape(MLA_KV_RANK, H * LANES).astype(BF16)
    wuv = jnp.pad(wkv[:, :, MLA_NOPE:], ((0, 0), (0, 0), (0, LANES - MLA_V)))
    wuv = wuv.reshape(MLA_KV_RANK, H * LANES).astype(BF16)
    pad_gain = lambda g: jnp.pad(g.reshape(1, MLA_QK), ((0, 0), (0, LANES - MLA_QK)))
    inv_freq = ROPE_THETA ** (-jnp.arange(0, MLA_ROPE, 2, dtype=F32) / MLA_ROPE)
    invf = jnp.concatenate([jnp.zeros((MLA_NOPE,), F32), inv_freq, inv_freq,
                            jnp.zeros((LANES - MLA_QK,), F32)]).reshape(1, LANES)
    pos = positions.astype(F32).reshape(B, S, 1)
    const = lambda shape: pl.BlockSpec(shape, lambda b, i: (0,) * len(shape))
    return pl.pallas_call(
        _mla_prep_kernel,
        grid=(B, S // tm),
        in_specs=[pl.BlockSpec((1, tm, n_lat), lambda b, i: (b, i, 0)),
                  pl.BlockSpec((1, tm, 1), lambda b, i: (b, i, 0)),
                  const((1, LANES)), const((1, MLA_Q_RANK)), const((1, MLA_KV_RANK)),
                  const((MLA_Q_RANK, H * LANES)), const((MLA_KV_RANK, H * LANES)),
                  const((MLA_KV_RANK, H * LANES)), const((1, LANES)), const((1, LANES))],
        out_specs=[pl.BlockSpec((1, tm, H * LANES), lambda b, i: (b, i, 0)),
                   pl.BlockSpec((1, tm, H * LANES), lambda b, i: (b, i, 0)),
                   pl.BlockSpec((1, tm, H * LANES), lambda b, i: (b, i, 0))],
        out_shape=[jax.ShapeDtypeStruct((B, S, H * LANES), BF16)] * 3,
        compiler_params=_params("parallel", "parallel"),
        name="mla_prep",
    )(lat, pos, invf, q_norm_gain.reshape(1, -1), kv_norm_gain.reshape(1, -1), wuq, wuk, wuv,
      pad_gain(q_head_gain), pad_gain(k_head_gain))


def _flash_kernel(q_ref, k_ref, v_ref, o_ref, *, tq):
    qi = pl.program_id(2)
    q = q_ref[0]
    q0, q1 = q[:, :LANES], q[:, LANES:]
    row = lax.broadcasted_iota(jnp.int32, (tq, tq), 0)
    col = lax.broadcasted_iota(jnp.int32, (tq, tq), 1)
    diag_mask = row >= col

    def online(s, m, acc, vb):
        m_new = jnp.maximum(m, jnp.max(s, axis=-1, keepdims=True))
        p = jnp.exp(s - m_new)
        return m_new, jnp.exp(m - m_new) * acc + _mm(p, vb)

    def block(kj, carry, masked):
        m0, m1, acc0, acc1 = carry
        start = pl.multiple_of(kj * tq, tq)
        kb = k_ref[0, pl.ds(start, tq), :]
        vb = v_ref[0, pl.ds(start, tq), :]
        s0 = _mm_nt(q0, kb[:, :LANES])
        s1 = _mm_nt(q1, kb[:, LANES:])
        if masked:
            s0 = jnp.where(diag_mask, s0, NEG_INF)
            s1 = jnp.where(diag_mask, s1, NEG_INF)
        m0, acc0 = online(s0, m0, acc0, vb[:, :LANES])
        m1, acc1 = online(s1, m1, acc1, vb[:, LANES:])
        return m0, m1, acc0, acc1

    neg = jnp.full((tq, 1), NEG_INF, F32)
    zero = jnp.zeros((tq, LANES), F32)
    carry = lax.fori_loop(0, qi, lambda kj, c: block(kj, c, False), (neg, neg, zero, zero))
    _, _, acc0, acc1 = block(qi, carry, True)
    first = lax.broadcasted_iota(jnp.int32, (tq, LANES), 1) < MLA_V
    num = jnp.where(first, acc0, pltpu.roll(acc1, MLA_V, 1))
    den = jnp.where(first, pltpu.roll(acc0, MLA_V, 1), acc1)
    o_ref[0] = (num / den).astype(o_ref.dtype)


def _flash_call(q, k, v):
    B, S, _ = q.shape
    tq = min(512, S)
    pairs = MLA_HEADS // 2
    return pl.pallas_call(
        functools.partial(_flash_kernel, tq=tq),
        grid=(B, pairs, S // tq),
        in_specs=[pl.BlockSpec((1, tq, 2 * LANES), lambda b, p, i: (b, i, p)),
                  pl.BlockSpec((1, S, 2 * LANES), lambda b, p, i: (b, 0, p)),
                  pl.BlockSpec((1, S, 2 * LANES), lambda b, p, i: (b, 0, p))],
        out_specs=pl.BlockSpec((1, tq, 2 * MLA_V), lambda b, p, i: (b, i, p)),
        out_shape=jax.ShapeDtypeStruct((B, S, MLA_HEADS * MLA_V), BF16),
        compiler_params=_params("parallel", "parallel", "arbitrary"),
        name="flash",
    )(q, k, v)


def _outproj_kernel(*refs, n_in):
    ins, (x_ref, gate_ref, w_ref, o_ref) = refs[:n_in], refs[n_in:]
    y = None
    off = 0
    for r in ins:
        n = r.shape[-1]
        part = _mm(r[0], w_ref[off:off + n, :])
        y = part if y is None else y + part
        off += n
    o_ref[0] = x_ref[0] + gate_ref[0] * y


def _outproj_call(ins, x, gate, w):
    B, S, D = x.shape
    tm = min(512, S)
    n_in = len(ins)
    return pl.pallas_call(
        functools.partial(_outproj_kernel, n_in=n_in),
        grid=(B, S // tm),
        in_specs=[pl.BlockSpec((1, tm, a.shape[-1]), lambda b, i: (b, i, 0)) for a in ins]
        + [pl.BlockSpec((1, tm, D), lambda b, i: (b, i, 0)),
           pl.BlockSpec((1, 1, D), lambda b, i: (b, 0, 0)),
           pl.BlockSpec(w.shape, lambda b, i: (0, 0))],
        out_specs=pl.BlockSpec((1, tm, D), lambda b, i: (b, i, 0)),
        out_shape=jax.ShapeDtypeStruct((B, S, D), F32),
        compiler_params=_params("parallel", "parallel"),
        name="outproj",
    )(*ins, x, gate, w)


def _mlp_kernel(x_ref, sh_ref, sc_ref, gate_ref, wu_ref, wd_ref, o_ref, h_ref, acc_ref):
    f = pl.program_id(2)

    @pl.when(f == 0)
    def _first():
        h_ref[...] = _adaln(x_ref[0], sh_ref[0], sc_ref[0]).astype(BF16)
        acc_ref[...] = jnp.zeros_like(acc_ref)

    u = jnp.maximum(jnp.dot(h_ref[...], wu_ref[...], preferred_element_type=F32), 0.0)
    acc_ref[...] += _mm(u * u, wd_ref[...])

    @pl.when(f == pl.num_programs(2) - 1)
    def _last():
        o_ref[0] = x_ref[0] + gate_ref[0] * acc_ref[...]


def _mlp_call(x, shift, scale, gate, w_up, w_down):
    B, S, D = x.shape
    F = w_up.shape[1]
    tm = min(1024, S)
    tf = 512
    vec = pl.BlockSpec((1, 1, D), lambda b, i, f: (b, 0, 0))
    return pl.pallas_call(
        _mlp_kernel,
        grid=(B, S // tm, F // tf),
        in_specs=[pl.BlockSpec((1, tm, D), lambda b, i, f: (b, i, 0)), vec, vec, vec,
                  pl.BlockSpec((D, tf), lambda b, i, f: (0, f)),
                  pl.BlockSpec((tf, D), lambda b, i, f: (f, 0))],
        out_specs=pl.BlockSpec((1, tm, D), lambda b, i, f: (b, i, 0)),
        out_shape=jax.ShapeDtypeStruct((B, S, D), F32),
        scratch_shapes=[pltpu.VMEM((tm, D), BF16), pltpu.VMEM((tm, D), F32)],
        compiler_params=_params("parallel", "parallel", "arbitrary"),
        name="mlp",
    )(x, shift, scale, gate, w_up, w_down)


def _rwkv_proj_kernel(x_ref, xp_ref, sh_ref, sc_ref, mu_ref, wr_ref, wk_ref, wv_ref,
                      w1_ref, w2_ref, a1_ref, a2_ref, g1_ref, g2_ref, w0_ref, a0_ref,
                      r_ref, w_ref, k_ref, v_ref, a_ref, g_ref):
    h = _adaln(x_ref[0], sh_ref[0], sc_ref[0])
    tm = h.shape[0]
    prev_tail = _adaln(xp_ref[0], sh_ref[0], sc_ref[0])[SUBLANES - 1:SUBLANES, :]
    prev_tail = jnp.where(pl.program_id(1) == 0, 0.0, prev_tail)
    row = lax.broadcasted_iota(jnp.int32, h.shape, 0)
    h_prev = jnp.where(row == 0, prev_tail, pltpu.roll(h, 1, 0))
    xx = h_prev - h
    mix = lambda j: h + xx * mu_ref[j:j + 1, :]
    xr, xw, xk, xv, xa, xg = (mix(j) for j in range(6))
    r_ref[0] = _mm(xr, wr_ref[...])
    k_ref[0] = _mm(xk, wk_ref[...])
    v_ref[0] = _mm(xv, wv_ref[...])
    wl = w0_ref[...] + _mm(jnp.tanh(_mm(xw, w1_ref[...])), w2_ref[...])
    w_log = -_softplus(-wl) - 0.5
    w_ref[0] = -jnp.exp(w_log)
    a_ref[0] = jax.nn.sigmoid(a0_ref[...] + _mm(_mm(xa, a1_ref[...]), a2_ref[...]))
    g_ref[0] = _mm(jax.nn.sigmoid(_mm(xg, g1_ref[...])), g2_ref[...])


def _pad_lora(w_a, w_b):
    r = w_a.shape[1]
    rp = -(-r // LANES) * LANES
    return (jnp.pad(w_a, ((0, 0), (0, rp - r))).astype(BF16),
            jnp.pad(w_b, ((0, rp - r), (0, 0))).astype(BF16))


def _rwkv_proj_call(x, shift, scale, mu, w_r, w_k, w_v, w1, w2, a1, a2, g1, g2, w0, a0):
    B, S, D = x.shape
    tm = min(256, S)
    w1p, w2p = _pad_lora(w1, w2)
    a1p, a2p = _pad_lora(a1, a2)
    g1p, g2p = _pad_lora(g1, g2)
    const = lambda t: pl.BlockSpec(t.shape, lambda b, i: (0,) * t.ndim)
    vec = pl.BlockSpec((1, 1, D), lambda b, i: (b, 0, 0))
    row_blocks = tm // SUBLANES
    weights = (mu, w_r.astype(BF16), w_k.astype(BF16), w_v.astype(BF16), w1p, w2p, a1p, a2p,
               g1p, g2p, w0.reshape(1, D), a0.reshape(1, D))
    out = pl.BlockSpec((1, tm, D), lambda b, i: (b, i, 0))
    return pl.pallas_call(
        _rwkv_proj_kernel,
        grid=(B, S // tm),
        in_specs=[pl.BlockSpec((1, tm, D), lambda b, i: (b, i, 0)),
                  pl.BlockSpec((1, SUBLANES, D),
                               lambda b, i: (b, jnp.maximum(i * row_blocks - 1, 0), 0)),
                  vec, vec] + [const(t) for t in weights],
        out_specs=[out] * 6,
        out_shape=[jax.ShapeDtypeStruct((B, S, D), F32)] * 6,
        compiler_params=_params("parallel", "parallel"),
        name="rwkv_proj",
    )(x, x, shift, scale, *weights)


def _rwkv_kernel(r_ref, w_ref, k_ref, v_ref, a_ref, g_ref, kk_ref, ka_ref, rk_ref, lng_ref, lnb_ref,
                 o_ref, st_ref):
    C, H, Dh = CHUNK, RWKV_HEADS, RWKV_HEAD_DIM

    @pl.when(pl.program_id(1) == 0)
    def _init():
        st_ref[...] = jnp.zeros_like(st_ref)

    incl, strict, eye = _tri_masks(C)
    w = w_ref[0]
    G = _tri_cumsum(incl.astype(BF16), w)
    g_last = G[C - 1:C, :]
    e_in = jnp.exp(G)
    e_ex = jnp.exp(G - w)
    e_neg = jnp.exp(-G)
    e_rest = jnp.exp(g_last - G)
    e_last = jnp.exp(g_last)
    r = r_ref[0]
    k0 = k_ref[0]
    v = v_ref[0]
    a = a_ref[0]
    g = g_ref[0]
    kk_raw = k0 * kk_ref[...]
    k_mod = k0 * (1.0 + (a - 1.0) * ka_ref[...])
    rk = rk_ref[...]
    lng = lng_ref[...]
    lnb = lnb_ref[...]

    heads = range(H)
    head = lambda t, h: t[:, h * Dh:(h + 1) * Dh]
    l2n = lambda t: t * lax.rsqrt(jnp.sum(t * t, axis=-1, keepdims=True) + 1e-12)
    kk = [l2n(head(kk_raw, h)) for h in heads]
    kh = [head(k_mod, h) for h in heads]
    rh = [head(r, h) for h in heads]
    vh = [head(v, h) for h in heads]
    bv = [kk[h] * head(a, h) for h in heads]
    a_t = [-kk[h] * head(e_ex, h) for h in heads]
    b_t = [bv[h] * head(e_neg, h) for h in heads]
    k_t = [kh[h] * head(e_neg, h) for h in heads]
    r_t = [rh[h] * head(e_in, h) for h in heads]
    S = [st_ref[h] for h in heads]
    a_ab = [jnp.where(strict, _mm_nt(a_t[h], b_t[h]), 0.0) for h in heads]
    a_ak = [jnp.where(strict, _mm_nt(a_t[h], k_t[h]), 0.0) for h in heads]
    a_rb = [jnp.where(incl, _mm_nt(r_t[h], b_t[h]), 0.0) for h in heads]
    a_rk = [jnp.where(incl, _mm_nt(r_t[h], k_t[h]), 0.0) for h in heads]
    rhs = [_mm_nt(a_t[h], S[h]) + _mm(a_ak[h], vh[h]) for h in heads]
    y_part = [_mm_nt(r_t[h], S[h]) + _mm(a_rk[h], vh[h]) for h in heads]
    T = _unit_lower_inv([-m for m in a_ab], eye)
    P = [_mm(T[h], rhs[h]) for h in heads]
    y = [y_part[h] + _mm(a_rb[h], P[h]) for h in heads]
    for h in heads:
        st_ref[h] = (S[h] * head(e_last, h) + _mm_tn(P[h], bv[h] * head(e_rest, h))
                     + _mm_tn(vh[h], kh[h] * head(e_rest, h)))
    for h in heads:
        mean = jnp.mean(y[h], axis=-1, keepdims=True)
        yc = y[h] - mean
        var = jnp.mean(yc * yc, axis=-1, keepdims=True)
        yn = yc * lax.rsqrt(var + RWKV_GN_EPS) * head(lng, h) + head(lnb, h)
        bonus = jnp.sum(rh[h] * kh[h] * head(rk, h), axis=-1, keepdims=True) * vh[h]
        o_ref[0, :, h * Dh:(h + 1) * Dh] = (yn + bonus) * head(g, h)


def _rwkv_call(r, w, k, v, a, g, k_k, k_a, r_k, ln_gain, ln_bias):
    B, S, D = r.shape
    C, H, Dh = CHUNK, RWKV_HEADS, RWKV_HEAD_DIM
    seq = pl.BlockSpec((1, C, D), lambda b, c: (b, c, 0))
    vec = pl.BlockSpec((1, D), lambda b, c: (0, 0))
    flat = lambda t: t.reshape(1, D)
    return pl.pallas_call(
        _rwkv_kernel,
        grid=(B, S // C),
        in_specs=[seq] * 6 + [vec] * 5,
        out_specs=seq,
        out_shape=jax.ShapeDtypeStruct((B, S, D), F32),
        scratch_shapes=[pltpu.VMEM((H, Dh, Dh), F32)],
        compiler_params=_params("parallel", "arbitrary"),
        name="rwkv",
    )(r, w, k, v, a, g, flat(k_k), flat(k_a), flat(r_k), flat(ln_gain), flat(ln_bias))


def kernel(x, c, positions, w_mod, b_mod, w_in0, gdn_conv_w, gdn_a_log, gdn_dt_bias, gdn_norm_gain, mla_q_norm_gain, mla_kv_norm_gain, mla_w_uq, mla_w_ukv, mla_q_head_gain, mla_k_head_gain, w_out0, rwkv_mu, rwkv_w_r, rwkv_w_k, rwkv_w_v, rwkv_w_o, rwkv_w0, rwkv_w1, rwkv_w2, rwkv_a0, rwkv_a1, rwkv_a2, rwkv_g1, rwkv_g2, rwkv_k_k, rwkv_k_a, rwkv_r_k, rwkv_ln_gain, rwkv_ln_bias, w_up, w_down):
    B, S, D = x.shape
    mod = _mod_call(c, w_mod, b_mod)

    def mod_chunks(layer):
        return [mod[layer, :, i * D:(i + 1) * D].reshape(B, 1, D) for i in range(6)]

    shift1, scale1, gate1, shift2, scale2, gate2 = mod_chunks(0)
    qkvz, lat = _inproj_call(x, shift1, scale1, _pack_w_in(w_in0[0]))
    o_gdn = _gdn_call(qkvz, lat, gdn_conv_w[0], gdn_a_log[0], gdn_dt_bias[0], gdn_norm_gain[0])
    q, k, v = _mla_prep_call(lat, positions, mla_q_norm_gain[0], mla_kv_norm_gain[0], mla_w_uq[0],
                             mla_w_ukv[0], mla_q_head_gain[0], mla_k_head_gain[0])
    o_mla = _flash_call(q, k, v)
    x = _outproj_call([o_gdn, o_mla], x, gate1, w_out0[0].astype(BF16))
    x = _mlp_call(x, shift2, scale2, gate2, w_up[0].astype(BF16), w_down[0].astype(BF16))

    shift1, scale1, gate1, shift2, scale2, gate2 = mod_chunks(1)
    r, w, k, v, a, g = _rwkv_proj_call(x, shift1, scale1, rwkv_mu[0], rwkv_w_r[0], rwkv_w_k[0],
                                       rwkv_w_v[0], rwkv_w1[0], rwkv_w2[0], rwkv_a1[0], rwkv_a2[0],
                                       rwkv_g1[0], rwkv_g2[0], rwkv_w0[0], rwkv_a0[0])
    y = _rwkv_call(r, w, k, v, a, g, rwkv_k_k[0], rwkv_k_a[0], rwkv_r_k[0], rwkv_ln_gain[0],
                   rwkv_ln_bias[0])
    x = _outproj_call([y], x, gate1, rwkv_w_o[0].astype(BF16))
    x = _mlp_call(x, shift2, scale2, gate2, w_up[1].astype(BF16), w_down[1].astype(BF16))
    return x
```

```python
import functools
import math

import jax
import jax.numpy as jnp
from jax import lax
from jax.experimental import pallas as pl
from jax.experimental.pallas import tpu as pltpu

F32 = jnp.float32
BF16 = jnp.bfloat16

D_MODEL = 1024
DEPTH = 2
GDN_HEADS = 8
GDN_HEAD_DIM = 64
GDN_WIDTH = GDN_HEADS * GDN_HEAD_DIM
GDN_CONV = 4
CHUNK = 64
CHUNKS_PER_STEP = 4
FLASH_UNROLL = 2
MLA_HEADS = 8
MLA_Q_RANK = 256
MLA_KV_RANK = 128
MLA_NOPE = 64
MLA_ROPE = 32
MLA_QK = MLA_NOPE + MLA_ROPE
MLA_V = 64
ROPE_THETA = 10000.0
NEG_INF = -1e30
RWKV_HEADS = 16
RWKV_HEAD_DIM = 64
RWKV_GN_EPS = 64e-5
D_FF = 4 * D_MODEL
EPS = 1e-6
LANES = 128
SUBLANES = 8
VMEM_LIMIT = 56 * 1024 * 1024


def _params(*sem):
    return pltpu.CompilerParams(dimension_semantics=sem, vmem_limit_bytes=VMEM_LIMIT)


def _mm(a, b):
    return jnp.dot(a.astype(BF16), b.astype(BF16), preferred_element_type=F32)


def _mm_nt(a, b):
    return lax.dot_general(a.astype(BF16), b.astype(BF16), (((1,), (1,)), ((), ())),
                           preferred_element_type=F32)


def _tri_cumsum(tri_bf16, x):
    hi = x.astype(BF16)
    r1 = x - hi.astype(F32)
    mid = r1.astype(BF16)
    lo = (r1 - mid.astype(F32)).astype(BF16)
    dot = functools.partial(jnp.dot, preferred_element_type=F32)
    return dot(tri_bf16, hi) + dot(tri_bf16, mid) + dot(tri_bf16, lo)


def _unit_lower_inv(Ls, eye):
    n = Ls[0].shape[0]
    row = lax.broadcasted_iota(jnp.int32, (n, n), 0)
    col = lax.broadcasted_iota(jnp.int32, (n, n), 1)
    Ts = [eye] * len(Ls)
    s = 1
    while s < n:
        lower_left = ((row // (2 * s)) == (col // (2 * s))) & ((row // s) % 2 == 1) & ((col // s) % 2 == 0)
        Cs = [jnp.where(lower_left, L, 0.0) for L in Ls]
        if s == 1:
            Ts = [T - C for T, C in zip(Ts, Cs)]
        else:
            Xs = [_mm(T, C) for T, C in zip(Ts, Cs)]
            Ts = [T - _mm(X, T) for T, X in zip(Ts, Xs)]
        s *= 2
    return Ts


def _pair_masks(n):
    row = lax.broadcasted_iota(jnp.int32, (n, 2 * n), 0)
    lane = lax.broadcasted_iota(jnp.int32, (n, 2 * n), 1)
    col = lane % n
    return row >= col, row > col, (row == col).astype(F32), lane < n


def _bd_rows(x, first):
    x = x.astype(BF16)
    zero = jnp.zeros_like(x)
    return jnp.concatenate([jnp.where(first, x, zero), jnp.where(first, zero, x)], axis=0)


def _unit_lower_inv_pairs(Ls, eye_p, first):
    n = Ls[0].shape[0]
    row = lax.broadcasted_iota(jnp.int32, (n, 2 * n), 0)
    col = lax.broadcasted_iota(jnp.int32, (n, 2 * n), 1) % n
    Ts = [eye_p] * len(Ls)
    s = 1
    while s < n:
        lower_left = ((row // (2 * s)) == (col // (2 * s))) & ((row // s) % 2 == 1) & ((col // s) % 2 == 0)
        Cs = [jnp.where(lower_left, L, 0.0) for L in Ls]
        if s == 1:
            Ts = [T - C for T, C in zip(Ts, Cs)]
        else:
            Xs = [_mm(T, _bd_rows(C, first)) for T, C in zip(Ts, Cs)]
            Ts = [T - _mm(X, _bd_rows(T, first)) for T, X in zip(Ts, Xs)]
        s *= 2
    return Ts


def _head_sums(x, ones_bd, split=False):
    hi = x.astype(BF16)
    out = jnp.dot(hi, ones_bd, preferred_element_type=F32)
    if split:
        out = out + jnp.dot((x - hi.astype(F32)).astype(BF16), ones_bd, preferred_element_type=F32)
    return out


def _softplus(x):
    return jnp.maximum(x, 0.0) + jnp.log1p(jnp.exp(-jnp.abs(x)))


def _silu(x):
    return x * jax.nn.sigmoid(x)


def _adaln(x, shift, scale):
    ms = jnp.mean(x * x, axis=-1, keepdims=True)
    return x * lax.rsqrt(ms + EPS) * (1.0 + scale) + shift


def _tri_masks(n):
    row = lax.broadcasted_iota(jnp.int32, (n, n), 0)
    col = lax.broadcasted_iota(jnp.int32, (n, n), 1)
    return row >= col, row > col, (row == col).astype(F32)


def _mod_kernel(c_ref, w_ref, b_ref, o_ref):
    cond = _silu(c_ref[...])
    o_ref[0] = jnp.dot(cond, w_ref[0], precision=lax.Precision.HIGHEST,
                       preferred_element_type=F32) + b_ref[0]


def _mod_call(c, w_mod, b_mod):
    B, D = c.shape
    depth, _, n = w_mod.shape
    tn = 1536
    c_pad = jnp.pad(c, ((0, SUBLANES - B), (0, 0)))
    out = pl.pallas_call(
        _mod_kernel,
        grid=(depth, n // tn),
        in_specs=[pl.BlockSpec((SUBLANES, D), lambda l, j: (0, 0)),
                  pl.BlockSpec((1, D, tn), lambda l, j: (l, 0, j)),
                  pl.BlockSpec((1, 1, tn), lambda l, j: (l, 0, j))],
        out_specs=pl.BlockSpec((1, SUBLANES, tn), lambda l, j: (l, 0, j)),
        out_shape=jax.ShapeDtypeStruct((depth, SUBLANES, n), F32),
        compiler_params=_params("parallel", "parallel"),
        name="mod",
    )(c_pad, w_mod, b_mod.reshape(depth, 1, n))
    return out[:, :B, :]


IN_PACKED = 2560


def _pack_w_in(w_in):
    D = w_in.shape[0]
    o_a = 4 * GDN_WIDTH
    o_dq = o_a + 2 * GDN_HEADS
    o_dkv = o_dq + MLA_Q_RANK
    o_kr = o_dkv + MLA_KV_RANK
    z = lambda n: jnp.zeros((D, n), w_in.dtype)
    return jnp.concatenate(
        [w_in[:, :o_a], w_in[:, o_dq:o_dkv], w_in[:, o_dkv:o_kr],
         w_in[:, o_a:o_dq], z(MLA_NOPE - 2 * GDN_HEADS), w_in[:, o_kr:o_kr + MLA_ROPE],
         z(LANES - MLA_NOPE - MLA_ROPE)], axis=1).astype(BF16)


def _inproj_kernel(x_ref, sh_ref, sc_ref, w_ref, qkvz_ref, lat_ref):
    h = _adaln(x_ref[0], sh_ref[0], sc_ref[0]).astype(BF16)
    y = jnp.dot(h, w_ref[...], preferred_element_type=F32)
    qkvz_ref[0] = y[:, :4 * GDN_WIDTH]
    lat_ref[0] = y[:, 4 * GDN_WIDTH:]


def _inproj_call(x, shift, scale, w_packed):
    B, S, D = x.shape
    tm = min(512, S)
    n_lat = IN_PACKED - 4 * GDN_WIDTH
    return pl.pallas_call(
        _inproj_kernel,
        grid=(B, S // tm),
        in_specs=[pl.BlockSpec((1, tm, D), lambda b, i: (b, i, 0)),
                  pl.BlockSpec((1, 1, D), lambda b, i: (b, 0, 0)),
                  pl.BlockSpec((1, 1, D), lambda b, i: (b, 0, 0)),
                  pl.BlockSpec((D, IN_PACKED), lambda b, i: (0, 0))],
        out_specs=[pl.BlockSpec((1, tm, 4 * GDN_WIDTH), lambda b, i: (b, i, 0)),
                   pl.BlockSpec((1, tm, n_lat), lambda b, i: (b, i, 0))],
        out_shape=[jax.ShapeDtypeStruct((B, S, 4 * GDN_WIDTH), F32),
                   jax.ShapeDtypeStruct((B, S, n_lat), F32)],
        compiler_params=_params("parallel", "parallel"),
        name="inproj",
    )(x, shift, scale, w_packed)


def _gdn_kernel(x_ref, ab_ref, cw_ref, alog_ref, dtb_ref, gain_ref, o_ref, ext_ref, st_ref):
    C, H, Dh, W = CHUNK, GDN_HEADS, GDN_HEAD_DIM, GDN_WIDTH

    @pl.when(pl.program_id(1) == 0)
    def _init():
        ext_ref[0:SUBLANES, :] = jnp.zeros((SUBLANES, 3 * W), F32)
        st_ref[...] = jnp.zeros_like(st_ref)

    R = x_ref.shape[1]
    x = x_ref[0]
    ext_ref[SUBLANES:SUBLANES + R, :] = x[:, :3 * W]
    base = SUBLANES - (GDN_CONV - 1)
    conv = cw_ref[0:1, :] * ext_ref[base:base + R, :]
    for j in range(1, GDN_CONV):
        conv = conv + cw_ref[j:j + 1, :] * ext_ref[base + j:base + j + R, :]
    ext_ref[0:SUBLANES, :] = x[R - SUBLANES:R, :3 * W]
    qkv = _silu(conv)

    causal, strict, eye = _tri_masks(C)
    tri = causal.astype(BF16)
    ab = ab_ref[0]
    beta_all = jax.nn.sigmoid(ab)
    g_all = -jnp.exp(alog_ref[...]) * _softplus(ab + dtb_ref[...])
    gain = gain_ref[...]
    heads = range(H)
    l2n = lambda t: t * lax.rsqrt(jnp.sum(t * t, axis=-1, keepdims=True) + 1e-12)

    units = []
    for c in range(R // C):
        rows = slice(c * C, (c + 1) * C)
        gcum = _tri_cumsum(tri, g_all[rows])
        gcum_t = gcum.T
        eg = jnp.exp(gcum)
        glast = gcum[C - 1:C, :]
        eg_rest = jnp.exp(glast - gcum)
        eg_last = jnp.exp(glast)
        for h in heads:
            head = lambda off: qkv[rows, off + h * Dh:off + (h + 1) * Dh]
            q = l2n(head(0)) * (Dh ** -0.5)
            k = l2n(head(W))
            bc = beta_all[rows, H + h:H + h + 1]
            kb = k * bc
            decay = jnp.where(causal, jnp.exp(jnp.where(causal, gcum[:, h:h + 1] - gcum_t[h:h + 1, :],
                                                        0.0)), 0.0)
            units.append(dict(
                c=c, h=h, decay=decay, vb=head(2 * W) * bc, k=k,
                kbq=jnp.concatenate([kb, q], axis=0),
                kbq_g=jnp.concatenate([kb * eg[:, h:h + 1], q * eg[:, h:h + 1]], axis=0),
                k_rest=k * eg_rest[:, h:h + 1], e_last=eg_last[:, h:h + 1]))
    for u in units:
        kk = _mm_nt(u["kbq"], u["k"])
        u["L"] = jnp.where(strict, kk[:C] * u["decay"], 0.0)
        u["qk"] = jnp.where(causal, kk[C:] * u["decay"], 0.0)
        u["k_rest_t"] = _mm_nt(eye, u["k_rest"])
    for u, T in zip(units, _unit_lower_inv([u["L"] for u in units], eye)):
        u["T"] = T

    S = [st_ref[h] for h in heads]
    for c in range(R // C):
        cu = units[c * H:(c + 1) * H]
        ks = [_mm(u["kbq_g"], S[u["h"]]) for u in cu]
        v_new = [_mm(u["T"], u["vb"] - s[:C]) for u, s in zip(cu, ks)]
        o = [s[C:] + _mm(u["qk"], vn) for u, s, vn in zip(cu, ks, v_new)]
        S = [S[u["h"]] * u["e_last"] + _mm(u["k_rest_t"], vn) for u, vn in zip(cu, v_new)]
        for h in heads:
            oh = o[h] * lax.rsqrt(jnp.mean(o[h] * o[h], axis=-1, keepdims=True) + EPS)
            z = x[c * C:(c + 1) * C, 3 * W + h * Dh:3 * W + (h + 1) * Dh]
            o_ref[0, c * C:(c + 1) * C, h * Dh:(h + 1) * Dh] = oh * gain * _silu(z)
    for h in heads:
        st_ref[h] = S[h]


def _gdn_call(qkvz, lat, conv_w, a_log, dt_bias, gain):
    B, S, _ = qkvz.shape
    H, Dh, W = GDN_HEADS, GDN_HEAD_DIM, GDN_WIDTH
    C = CHUNK * CHUNKS_PER_STEP
    lane_pad = lambda t: jnp.pad(t.reshape(1, H), ((0, 0), (0, LANES - H)))
    ab_block = (lat.shape[-1] // LANES) - 1
    return pl.pallas_call(
        _gdn_kernel,
        grid=(B, S // C),
        in_specs=[pl.BlockSpec((1, C, 4 * W), lambda b, c: (b, c, 0)),
                  pl.BlockSpec((1, C, LANES), lambda b, c: (b, c, ab_block)),
                  pl.BlockSpec((GDN_CONV, 3 * W), lambda b, c: (0, 0)),
                  pl.BlockSpec((1, LANES), lambda b, c: (0, 0)),
                  pl.BlockSpec((1, LANES), lambda b, c: (0, 0)),
                  pl.BlockSpec((1, Dh), lambda b, c: (0, 0))],
        out_specs=pl.BlockSpec((1, C, W), lambda b, c: (b, c, 0)),
        out_shape=jax.ShapeDtypeStruct((B, S, W), F32),
        scratch_shapes=[pltpu.VMEM((C + SUBLANES, 3 * W), F32),
                        pltpu.VMEM((H, Dh, Dh), F32)],
        compiler_params=_params("parallel", "arbitrary"),
        name="gdn",
    )(qkvz, lat, conv_w, lane_pad(a_log), lane_pad(dt_bias), gain.reshape(1, Dh))


def _mla_prep_kernel(lat_ref, pos_ref, invf_ref, qg_ref, kvg_ref, wuq_ref, wuk_ref, wuv_ref,
                     qhg_ref, khg_ref, q_ref, k_ref, v_ref):
    lat = lat_ref[0]
    dq = lat[:, :MLA_Q_RANK]
    dkv = lat[:, MLA_Q_RANK:MLA_Q_RANK + MLA_KV_RANK]
    slab = lat[:, MLA_Q_RANK + MLA_KV_RANK:]
    q_lat = dq * lax.rsqrt(jnp.mean(dq * dq, axis=-1, keepdims=True) + EPS) * qg_ref[...]
    kv_lat = dkv * lax.rsqrt(jnp.mean(dkv * dkv, axis=-1, keepdims=True) + EPS) * kvg_ref[...]
    qm = _mm(q_lat, wuq_ref[...])
    kn = _mm(kv_lat, wuk_ref[...])
    v_row = lax.broadcasted_iota(jnp.int32, (MLA_HEADS * LANES, 1), 0)
    v_ones = ((v_row % LANES) >= MLA_V).astype(F32)
    v_ref[0] = (_mm_nt(wuv_ref[...], kv_lat) + v_ones).astype(BF16)

    lane = lax.broadcasted_iota(jnp.int32, slab.shape, 1)
    half = MLA_ROPE // 2
    ang = pos_ref[0] * invf_ref[...]
    cosf = jnp.cos(ang)
    sinf = jnp.sin(ang)
    sin_signed = jnp.where(lane < MLA_NOPE + half, -sinf, sinf)
    k_rope = jnp.where((lane >= MLA_NOPE) & (lane < MLA_QK), slab, 0.0)

    def head_norm_rope(t, gain):
        t = t * lax.rsqrt(jnp.sum(t * t, axis=-1, keepdims=True) * (1.0 / MLA_QK) + EPS) * gain
        partner = jnp.where(lane < MLA_NOPE + half,
                            pltpu.roll(t, LANES - half, 1), pltpu.roll(t, half, 1))
        return t * cosf + partner * sin_signed

    scale = MLA_QK ** -0.5
    for h in range(MLA_HEADS):
        sl = slice(h * LANES, (h + 1) * LANES)
        q_ref[0, :, sl] = (head_norm_rope(qm[:, sl], qhg_ref[...]) * scale).astype(BF16)
        k_ref[0, :, sl] = head_norm_rope(kn[:, sl] + k_rope, khg_ref[...]).astype(BF16)


def _mla_prep_call(lat, positions, q_norm_gain, kv_norm_gain, w_uq, w_ukv, q_head_gain, k_head_gain):
    B, S, n_lat = lat.shape
    H = MLA_HEADS
    tm = min(512, S)
    wuq = jnp.pad(w_uq.reshape(MLA_Q_RANK, H, MLA_QK), ((0, 0), (0, 0), (0, LANES - MLA_QK)))
    wuq = wuq.reshape(MLA_Q_RANK, H * LANES).astype(BF16)
    wkv = w_ukv.reshape(MLA_KV_RANK, H, MLA_NOPE + MLA_V)
    wuk = jnp.pad(wkv[:, :, :MLA_NOPE], ((0, 0), (0, 0), (0, LANES - MLA_NOPE)))
    wuk = wuk.reshape(MLA_KV_RANK, H * LANES).astype(BF16)
    wuv = jnp.pad(wkv[:, :, MLA_NOPE:], ((0, 0), (0, 0), (0, LANES - MLA_V)))
    wuv = wuv.reshape(MLA_KV_RANK, H * LANES).T.astype(BF16)
    pad_gain = lambda g: jnp.pad(g.reshape(1, MLA_QK), ((0, 0), (0, LANES - MLA_QK)))
    inv_freq = ROPE_THETA ** (-jnp.arange(0, MLA_ROPE, 2, dtype=F32) / MLA_ROPE)
    invf = jnp.concatenate([jnp.zeros((MLA_NOPE,), F32), inv_freq, inv_freq,
                            jnp.zeros((LANES - MLA_QK,), F32)]).reshape(1, LANES)
    pos = positions.astype(F32).reshape(B, S, 1)
    const = lambda shape: pl.BlockSpec(shape, lambda b, i: (0,) * len(shape))
    return pl.pallas_call(
        _mla_prep_kernel,
        grid=(B, S // tm),
        in_specs=[pl.BlockSpec((1, tm, n_lat), lambda b, i: (b, i, 0)),
                  pl.BlockSpec((1, tm, 1), lambda b, i: (b, i, 0)),
                  const((1, LANES)), const((1, MLA_Q_RANK)), const((1, MLA_KV_RANK)),
                  const((MLA_Q_RANK, H * LANES)), const((MLA_KV_RANK, H * LANES)),
                  const((H * LANES, MLA_KV_RANK)), const((1, LANES)), const((1, LANES))],
        out_specs=[pl.BlockSpec((1, tm, H * LANES), lambda b, i: (b, i, 0)),
                   pl.BlockSpec((1, tm, H * LANES), lambda b, i: (b, i, 0)),
                   pl.BlockSpec((1, H * LANES, tm), lambda b, i: (b, 0, i))],
        out_shape=[jax.ShapeDtypeStruct((B, S, H * LANES), BF16)] * 2
        + [jax.ShapeDtypeStruct((B, H * LANES, S), BF16)],
        compiler_params=_params("parallel", "parallel"),
        name="mla_prep",
    )(lat, pos, invf, q_norm_gain.reshape(1, -1), kv_norm_gain.reshape(1, -1), wuq, wuk, wuv,
      pad_gain(q_head_gain), pad_gain(k_head_gain))


def _flash_kernel(q_ref, k_ref, vt_ref, o_ref, *, tq):
    qi = pl.program_id(2)
    q = q_ref[0]
    q0, q1 = q[:, :LANES], q[:, LANES:]
    key = lax.broadcasted_iota(jnp.int32, (tq, tq), 0)
    qry = lax.broadcasted_iota(jnp.int32, (tq, tq), 1)
    diag_mask = key <= qry

    def scores(kj):
        start = pl.multiple_of(kj * tq, tq)
        rows = pl.ds(start, tq)
        return (_mm_nt(k_ref[0, rows, :LANES], q0),
                _mm_nt(k_ref[0, rows, LANES:], q1))

    def online(st, m, acc, vt):
        m_new = jnp.maximum(m, jnp.max(st, axis=0, keepdims=True))
        pt = jnp.exp(st - m_new)
        return m_new, jnp.exp(m - m_new) * acc + _mm(vt, pt)

    def softmax_pv(kj, s0, s1, m0, m1, acc0, acc1):
        start = pl.multiple_of(kj * tq, tq)
        cols = pl.ds(start, tq)
        m0, acc0 = online(s0, m0, acc0, vt_ref[0, :LANES, cols])
        m1, acc1 = online(s1, m1, acc1, vt_ref[0, LANES:, cols])
        return m0, m1, acc0, acc1

    def group(t, carry):
        blocks = [FLASH_UNROLL * t + u for u in range(FLASH_UNROLL)]
        ss = [scores(kj) for kj in blocks]
        for kj, s in zip(blocks, ss):
            carry = softmax_pv(kj, *s, *carry)
        return carry

    def single(kj, carry):
        return softmax_pv(kj, *scores(kj), *carry)

    neg = jnp.full((1, tq), NEG_INF, F32)
    zero = jnp.zeros((LANES, tq), F32)
    n_group = qi // FLASH_UNROLL
    carry = lax.fori_loop(0, n_group, group, (neg, neg, zero, zero))
    carry = lax.fori_loop(n_group * FLASH_UNROLL, qi, single, carry)
    s0, s1 = scores(qi)
    s0 = jnp.where(diag_mask, s0, NEG_INF)
    s1 = jnp.where(diag_mask, s1, NEG_INF)
    _, _, acc0, acc1 = softmax_pv(qi, s0, s1, *carry)
    out_t = jnp.concatenate([acc0[:MLA_V] / acc0[MLA_V:MLA_V + 1],
                             acc1[:MLA_V] / acc1[MLA_V:MLA_V + 1]], axis=0)
    o_ref[0] = out_t.T.astype(o_ref.dtype)


def _flash_call(q, k, v):
    B, S, _ = q.shape
    tq = min(512, S)
    pairs = MLA_HEADS // 2
    return pl.pallas_call(
        functools.partial(_flash_kernel, tq=tq),
        grid=(B, pairs, S // tq),
        in_specs=[pl.BlockSpec((1, tq, 2 * LANES), lambda b, p, i: (b, i, p)),
                  pl.BlockSpec((1, S, 2 * LANES), lambda b, p, i: (b, 0, p)),
                  pl.BlockSpec((1, 2 * LANES, S), lambda b, p, i: (b, p, 0))],
        out_specs=pl.BlockSpec((1, tq, 2 * MLA_V), lambda b, p, i: (b, i, p)),
        out_shape=jax.ShapeDtypeStruct((B, S, MLA_HEADS * MLA_V), BF16),
        compiler_params=_params("parallel", "parallel", "arbitrary"),
        name="flash",
    )(q, k, v)


def _outproj_kernel(*refs, n_in):
    ins, (x_ref, gate_ref, w_ref, o_ref) = refs[:n_in], refs[n_in:]
    y = None
    off = 0
    for r in ins:
        n = r.shape[-1]
        part = _mm(r[0], w_ref[off:off + n, :])
        y = part if y is None else y + part
        off += n
    o_ref[0] = x_ref[0] + gate_ref[0] * y


def _outproj_call(ins, x, gate, w):
    B, S, D = x.shape
    tm = min(512, S)
    n_in = len(ins)
    return pl.pallas_call(
        functools.partial(_outproj_kernel, n_in=n_in),
        grid=(B, S // tm),
        in_specs=[pl.BlockSpec((1, tm, a.shape[-1]), lambda b, i: (b, i, 0)) for a in ins]
        + [pl.BlockSpec((1, tm, D), lambda b, i: (b, i, 0)),
           pl.BlockSpec((1, 1, D), lambda b, i: (b, 0, 0)),
           pl.BlockSpec(w.shape, lambda b, i: (0, 0))],
        out_specs=pl.BlockSpec((1, tm, D), lambda b, i: (b, i, 0)),
        out_shape=jax.ShapeDtypeStruct((B, S, D), F32),
        compiler_params=_params("parallel", "parallel"),
        name="outproj",
    )(*ins, x, gate, w)


def _mlp_kernel(x_ref, sh_ref, sc_ref, gate_ref, wu_ref, wd_ref, o_ref, h_ref, acc_ref):
    f = pl.program_id(2)

    @pl.when(f == 0)
    def _first():
        h_ref[...] = _adaln(x_ref[0], sh_ref[0], sc_ref[0]).astype(BF16)
        acc_ref[...] = jnp.zeros_like(acc_ref)

    u = jnp.maximum(jnp.dot(h_ref[...], wu_ref[...], preferred_element_type=F32), 0.0)
    acc_ref[...] += _mm(u * u, wd_ref[...])

    @pl.when(f == pl.num_programs(2) - 1)
    def _last():
        o_ref[0] = x_ref[0] + gate_ref[0] * acc_ref[...]


def _mlp_call(x, shift, scale, gate, w_up, w_down):
    B, S, D = x.shape
    F = w_up.shape[1]
    tm = min(1024, S)
    tf = 512
    vec = pl.BlockSpec((1, 1, D), lambda b, i, f: (b, 0, 0))
    return pl.pallas_call(
        _mlp_kernel,
        grid=(B, S // tm, F // tf),
        in_specs=[pl.BlockSpec((1, tm, D), lambda b, i, f: (b, i, 0)), vec, vec, vec,
                  pl.BlockSpec((D, tf), lambda b, i, f: (0, f)),
                  pl.BlockSpec((tf, D), lambda b, i, f: (f, 0))],
        out_specs=pl.BlockSpec((1, tm, D), lambda b, i, f: (b, i, 0)),
        out_shape=jax.ShapeDtypeStruct((B, S, D), F32),
        scratch_shapes=[pltpu.VMEM((tm, D), BF16), pltpu.VMEM((tm, D), F32)],
        compiler_params=_params("parallel", "parallel", "arbitrary"),
        name="mlp",
    )(x, shift, scale, gate, w_up, w_down)


def _rwkv_proj_kernel(x_ref, xp_ref, sh_ref, sc_ref, mu_ref, wr_ref, wk_ref, wv_ref,
                      w1_ref, w2_ref, a1_ref, a2_ref, g1_ref, g2_ref, w0_ref, a0_ref,
                      r_ref, w_ref, k_ref, v_ref, a_ref, g_ref):
    h = _adaln(x_ref[0], sh_ref[0], sc_ref[0])
    tm = h.shape[0]
    prev_tail = _adaln(xp_ref[0], sh_ref[0], sc_ref[0])[SUBLANES - 1:SUBLANES, :]
    prev_tail = jnp.where(pl.program_id(1) == 0, 0.0, prev_tail)
    row = lax.broadcasted_iota(jnp.int32, h.shape, 0)
    h_prev = jnp.where(row == 0, prev_tail, pltpu.roll(h, 1, 0))
    xx = h_prev - h
    mix = lambda j: h + xx * mu_ref[j:j + 1, :]
    xr, xw, xk, xv, xa, xg = (mix(j) for j in range(6))
    r_ref[0] = _mm(xr, wr_ref[...])
    k_ref[0] = _mm(xk, wk_ref[...])
    v_ref[0] = _mm(xv, wv_ref[...])
    wl = w0_ref[...] + _mm(jnp.tanh(_mm(xw, w1_ref[...])), w2_ref[...])
    w_log = -_softplus(-wl) - 0.5
    w_ref[0] = -jnp.exp(w_log)
    a_ref[0] = jax.nn.sigmoid(a0_ref[...] + _mm(_mm(xa, a1_ref[...]), a2_ref[...]))
    g_ref[0] = _mm(jax.nn.sigmoid(_mm(xg, g1_ref[...])), g2_ref[...])


def _pad_lora(w_a, w_b):
    r = w_a.shape[1]
    rp = -(-r // LANES) * LANES
    return (jnp.pad(w_a, ((0, 0), (0, rp - r))).astype(BF16),
            jnp.pad(w_b, ((0, rp - r), (0, 0))).astype(BF16))


def _rwkv_proj_call(x, shift, scale, mu, w_r, w_k, w_v, w1, w2, a1, a2, g1, g2, w0, a0):
    B, S, D = x.shape
    tm = min(256, S)
    w1p, w2p = _pad_lora(w1, w2)
    a1p, a2p = _pad_lora(a1, a2)
    g1p, g2p = _pad_lora(g1, g2)
    const = lambda t: pl.BlockSpec(t.shape, lambda b, i: (0,) * t.ndim)
    vec = pl.BlockSpec((1, 1, D), lambda b, i: (b, 0, 0))
    row_blocks = tm // SUBLANES
    weights = (mu, w_r.astype(BF16), w_k.astype(BF16), w_v.astype(BF16), w1p, w2p, a1p, a2p,
               g1p, g2p, w0.reshape(1, D), a0.reshape(1, D))
    out = pl.BlockSpec((1, tm, D), lambda b, i: (b, i, 0))
    return pl.pallas_call(
        _rwkv_proj_kernel,
        grid=(B, S // tm),
        in_specs=[pl.BlockSpec((1, tm, D), lambda b, i: (b, i, 0)),
                  pl.BlockSpec((1, SUBLANES, D),
                               lambda b, i: (b, jnp.maximum(i * row_blocks - 1, 0), 0)),
                  vec, vec] + [const(t) for t in weights],
        out_specs=[out] * 6,
        out_shape=[jax.ShapeDtypeStruct((B, S, D), F32)] * 6,
        compiler_params=_params("parallel", "parallel"),
        name="rwkv_proj",
    )(x, x, shift, scale, *weights)


def _rwkv_kernel(r_ref, w_ref, k_ref, v_ref, a_ref, g_ref, kk_ref, ka_ref, rk_ref, lng_ref, lnb_ref,
                 o_ref, st_ref):
    C, Dh = CHUNK, RWKV_HEAD_DIM
    W2 = 2 * Dh
    pairs = range(RWKV_HEADS // 2)

    @pl.when(pl.program_id(1) == 0)
    def _init():
        st_ref[...] = jnp.zeros_like(st_ref)

    R = r_ref.shape[1]
    incl, strict, eye_p, first = _pair_masks(C)
    tri = _tri_masks(C)[0].astype(BF16)
    r2 = lax.broadcasted_iota(jnp.int32, (W2, W2), 0)
    c2 = lax.broadcasted_iota(jnp.int32, (W2, W2), 1)
    same_head = (r2 // Dh) == (c2 // Dh)
    ones_bd = same_head.astype(BF16)
    eye2 = (r2 == c2).astype(BF16)
    inv_dh = 1.0 / Dh

    units = []
    for c in range(R // C):
        rows = slice(c * C, (c + 1) * C)
        w = w_ref[0, rows, :]
        G = _tri_cumsum(tri, w)
        g_last = G[C - 1:C, :]
        e_in = jnp.exp(G)
        e_ex = jnp.exp(G - w)
        e_neg = jnp.exp(-G)
        e_rest = jnp.exp(g_last - G)
        e_last_col = jnp.broadcast_to(jnp.exp(g_last), (SUBLANES, g_last.shape[1])).T
        r = r_ref[0, rows, :]
        k0 = k_ref[0, rows, :]
        a = a_ref[0, rows, :]
        kk_raw = k0 * kk_ref[...]
        k_mod = k0 * (1.0 + (a - 1.0) * ka_ref[...])
        for p in pairs:
            sl = slice(p * W2, (p + 1) * W2)
            kkr = kk_raw[:, sl]
            kk = kkr * lax.rsqrt(_head_sums(kkr * kkr, ones_bd, split=True) + 1e-12)
            kh, rh, vh = k_mod[:, sl], r[:, sl], v_ref[0, rows, sl]
            bv = kk * a[:, sl]
            units.append(dict(
                c=c, sl=sl, rh=rh, kh=kh, vh=vh,
                ar=jnp.concatenate([-kk * e_ex[:, sl], rh * e_in[:, sl]], axis=0),
                b_t=bv * e_neg[:, sl], k_t=kh * e_neg[:, sl],
                bk_rest=jnp.concatenate([bv * e_rest[:, sl], kh * e_rest[:, sl]], axis=0),
                e_last=e_last_col[sl, 0:1]))
    for u in units:
        ab = _mm_nt(u["ar"], _bd_rows(u["b_t"], first))
        ak = _mm_nt(u["ar"], _bd_rows(u["k_t"], first))
        u["a_ab"] = jnp.where(strict, ab[:C], 0.0)
        u["a_rb"] = jnp.where(incl, ab[C:], 0.0)
        u["akrk"] = jnp.concatenate([jnp.where(strict, ak[:C], 0.0), jnp.where(incl, ak[C:], 0.0)], axis=0)
        u["bk_rest_t"] = _mm_nt(eye2, u["bk_rest"])
    for u in units:
        u["kv"] = _mm(u["akrk"], _bd_rows(u["vh"], first))
        u["bonus"] = _head_sums(u["rh"] * u["kh"] * rk_ref[:, u["sl"]], ones_bd) * u["vh"]
    for u, T in zip(units, _unit_lower_inv_pairs([-u["a_ab"] for u in units], eye_p, first)):
        u["T"] = T

    n_pairs = len(pairs)
    S = [st_ref[p] for p in pairs]
    for c in range(R // C):
        cu = units[c * n_pairs:(c + 1) * n_pairs]
        for p, u in zip(pairs, cu):
            u["sd"] = _mm(u["ar"], S[p]) + u["kv"]
        for u in cu:
            u["P"] = _mm(u["T"], _bd_rows(u["sd"][:C], first))
        S = [S[p] * u["e_last"]
             + jnp.where(same_head, _mm(u["bk_rest_t"], jnp.concatenate([u["P"], u["vh"]], axis=0)), 0.0)
             for p, u in zip(pairs, cu)]
    for p in pairs:
        st_ref[p] = S[p]

    for u in units:
        u["y"] = u["sd"][C:] + _mm(u["a_rb"], _bd_rows(u["P"], first))
    for u in units:
        u["yc"] = u["y"] - _head_sums(u["y"], ones_bd) * inv_dh
    for u in units:
        sl = u["sl"]
        rows = slice(u["c"] * C, (u["c"] + 1) * C)
        var = _head_sums(u["yc"] * u["yc"], ones_bd) * inv_dh
        yn = u["yc"] * lax.rsqrt(var + RWKV_GN_EPS) * lng_ref[:, sl] + lnb_ref[:, sl]
        o_ref[0, rows, sl] = (yn + u["bonus"]) * g_ref[0, rows, sl]


def _rwkv_call(r, w, k, v, a, g, k_k, k_a, r_k, ln_gain, ln_bias):
    B, S, D = r.shape
    H, Dh = RWKV_HEADS, RWKV_HEAD_DIM
    C = CHUNK * CHUNKS_PER_STEP
    seq = pl.BlockSpec((1, C, D), lambda b, c: (b, c, 0))
    vec = pl.BlockSpec((1, D), lambda b, c: (0, 0))
    flat = lambda t: t.reshape(1, D)
    return pl.pallas_call(
        _rwkv_kernel,
        grid=(B, S // C),
        in_specs=[seq] * 6 + [vec] * 5,
        out_specs=seq,
        out_shape=jax.ShapeDtypeStruct((B, S, D), F32),
        scratch_shapes=[pltpu.VMEM((H // 2, 2 * Dh, 2 * Dh), F32)],
        compiler_params=_params("parallel", "arbitrary"),
        name="rwkv",
    )(r, w, k, v, a, g, flat(k_k), flat(k_a), flat(r_k), flat(ln_gain), flat(ln_bias))


def kernel(x, c, positions, w_mod, b_mod, w_in0, gdn_conv_w, gdn_a_log, gdn_dt_bias, gdn_norm_gain, mla_q_norm_gain, mla_kv_norm_gain, mla_w_uq, mla_w_ukv, mla_q_head_gain, mla_k_head_gain, w_out0, rwkv_mu, rwkv_w_r, rwkv_w_k, rwkv_w_v, rwkv_w_o, rwkv_w0, rwkv_w1, rwkv_w2, rwkv_a0, rwkv_a1, rwkv_a2, rwkv_g1, rwkv_g2, rwkv_k_k, rwkv_k_a, rwkv_r_k, rwkv_ln_gain, rwkv_ln_bias, w_up, w_down):
    B, S, D = x.shape
    mod = _mod_call(c, w_mod, b_mod)

    def mod_chunks(layer):
        return [mod[layer, :, i * D:(i + 1) * D].reshape(B, 1, D) for i in range(6)]

    shift1, scale1, gate1, shift2, scale2, gate2 = mod_chunks(0)
    qkvz, lat = _inproj_call(x, shift1, scale1, _pack_w_in(w_in0[0]))
    o_gdn = _gdn_call(qkvz, lat, gdn_conv_w[0], gdn_a_log[0], gdn_dt_bias[0], gdn_norm_gain[0])
    q, k, v = _mla_prep_call(lat, positions, mla_q_norm_gain[0], mla_kv_norm_gain[0], mla_w_uq[0],
                             mla_w_ukv[0], mla_q_head_gain[0], mla_k_head_gain[0])
    o_mla = _flash_call(q, k, v)
    x = _outproj_call([o_gdn, o_mla], x, gate1, w_out0[0].astype(BF16))
    x = _mlp_call(x, shift2, scale2, gate2, w_up[0].astype(BF16), w_down[0].astype(BF16))

    shift1, scale1, gate1, shift2, scale2, gate2 = mod_chunks(1)
    r, w, k, v, a, g = _rwkv_proj_call(x, shift1, scale1, rwkv_mu[0], rwkv_w_r[0], rwkv_w_k[0],
                                       rwkv_w_v[0], rwkv_w1[0], rwkv_w2[0], rwkv_a1[0], rwkv_a2[0],
                                       rwkv_g1[0], rwkv_g2[0], rwkv_w0[0], rwkv_a0[0])
    y = _rwkv_call(r, w, k, v, a, g, rwkv_k_k[0], rwkv_k_a[0], rwkv_r_k[0], rwkv_ln_gain[0],
                   rwkv_ln_bias[0])
    x = _outproj_call([y], x, gate1, rwkv_w_o[0].astype(BF16))
    x = _mlp_call(x, shift2, scale2, gate2, w_up[1].astype(BF16), w_down[1].astype(BF16))
    return x
```

```python
import functools
import math

import jax
import jax.numpy as jnp
from jax import lax
from jax.experimental import pallas as pl
from jax.experimental.pallas import tpu as pltpu

F32 = jnp.float32
BF16 = jnp.bfloat16

D_MODEL = 1024
DEPTH = 2
GDN_HEADS = 8
GDN_HEAD_DIM = 64
GDN_WIDTH = GDN_HEADS * GDN_HEAD_DIM
GDN_CONV = 4
CHUNK = 64
CHUNKS_PER_STEP = 4
FLASH_TK = 512
FLASH_UNROLL = 2
MLA_HEADS = 8
MLA_Q_RANK = 256
MLA_KV_RANK = 128
MLA_NOPE = 64
MLA_ROPE = 32
MLA_QK = MLA_NOPE + MLA_ROPE
MLA_V = 64
ROPE_THETA = 10000.0
NEG_INF = -1e30
RWKV_HEADS = 16
RWKV_HEAD_DIM = 64
RWKV_GN_EPS = 64e-5
D_FF = 4 * D_MODEL
EPS = 1e-6
LANES = 128
SUBLANES = 8
VMEM_LIMIT = 56 * 1024 * 1024


def _params(*sem):
    return pltpu.CompilerParams(dimension_semantics=sem, vmem_limit_bytes=VMEM_LIMIT)


def _mm(a, b):
    return jnp.dot(a.astype(BF16), b.astype(BF16), preferred_element_type=F32)


def _mm_nt(a, b):
    return lax.dot_general(a.astype(BF16), b.astype(BF16), (((1,), (1,)), ((), ())),
                           preferred_element_type=F32)


def _tri_cumsum(tri_bf16, x):
    hi = x.astype(BF16)
    r1 = x - hi.astype(F32)
    mid = r1.astype(BF16)
    lo = (r1 - mid.astype(F32)).astype(BF16)
    dot = functools.partial(jnp.dot, preferred_element_type=F32)
    return dot(tri_bf16, hi) + dot(tri_bf16, mid) + dot(tri_bf16, lo)


def _unit_lower_inv(Ls, eye):
    n = Ls[0].shape[0]
    row = lax.broadcasted_iota(jnp.int32, (n, n), 0)
    col = lax.broadcasted_iota(jnp.int32, (n, n), 1)
    Ts = [eye] * len(Ls)
    s = 1
    while s < n:
        lower_left = ((row // (2 * s)) == (col // (2 * s))) & ((row // s) % 2 == 1) & ((col // s) % 2 == 0)
        Cs = [jnp.where(lower_left, L, 0.0) for L in Ls]
        if s == 1:
            Ts = [T - C for T, C in zip(Ts, Cs)]
        else:
            Xs = [_mm(T, C) for T, C in zip(Ts, Cs)]
            Ts = [T - _mm(X, T) for T, X in zip(Ts, Xs)]
        s *= 2
    return Ts


def _pair_masks(n):
    row = lax.broadcasted_iota(jnp.int32, (n, 2 * n), 0)
    lane = lax.broadcasted_iota(jnp.int32, (n, 2 * n), 1)
    col = lane % n
    return row >= col, row > col, (row == col).astype(F32), lane < n


def _bd_rows(x, first):
    x = x.astype(BF16)
    zero = jnp.zeros_like(x)
    return jnp.concatenate([jnp.where(first, x, zero), jnp.where(first, zero, x)], axis=0)


def _unit_lower_inv_pairs(units, eye_p, first):
    n = units[0]["L"].shape[0]
    row = lax.broadcasted_iota(jnp.int32, (n, 2 * n), 0)
    col = lax.broadcasted_iota(jnp.int32, (n, 2 * n), 1) % n
    Ts = [eye_p] * len(units)
    s = 1
    while s < n:
        lower_left = ((row // (2 * s)) == (col // (2 * s))) & ((row // s) % 2 == 1) & ((col // s) % 2 == 0)
        Cs = [jnp.where(lower_left, u["L"], 0.0) for u in units]
        if s == 1:
            Ts = [T - C for T, C in zip(Ts, Cs)]
        else:
            Xs = [_mm(T, _bd_rows(C, first)) for T, C in zip(Ts, Cs)]
            yield
            Ts = [T - _mm(X, _bd_rows(T, first)) for T, X in zip(Ts, Xs)]
            yield
        s *= 2
    for u, T in zip(units, Ts):
        u["T"] = T


def _emit(*gens):
    live = list(gens)
    while live:
        for g in list(live):
            try:
                next(g)
            except StopIteration:
                live.remove(g)


def _head_sums(xs, ones_bd, split=False):
    n = xs[0].shape[0]
    x = jnp.concatenate(xs, axis=0)
    hi = x.astype(BF16)
    out = jnp.dot(hi, ones_bd, preferred_element_type=F32)
    if split:
        out = out + jnp.dot((x - hi.astype(F32)).astype(BF16), ones_bd, preferred_element_type=F32)
    return [out[i * n:(i + 1) * n] for i in range(len(xs))]


def _softplus(x):
    return jnp.maximum(x, 0.0) + jnp.log1p(jnp.exp(-jnp.abs(x)))


def _silu(x):
    return x * jax.nn.sigmoid(x)


def _adaln(x, shift, scale):
    ms = jnp.mean(x * x, axis=-1, keepdims=True)
    return x * lax.rsqrt(ms + EPS) * (1.0 + scale) + shift


def _tri_masks(n):
    row = lax.broadcasted_iota(jnp.int32, (n, n), 0)
    col = lax.broadcasted_iota(jnp.int32, (n, n), 1)
    return row >= col, row > col, (row == col).astype(F32)


def _mod_kernel(c_ref, w_ref, b_ref, o_ref):
    cond = _silu(c_ref[...])
    o_ref[0] = jnp.dot(cond, w_ref[0], precision=lax.Precision.HIGHEST,
                       preferred_element_type=F32) + b_ref[0]


def _mod_call(c, w_mod, b_mod):
    B, D = c.shape
    depth, _, n = w_mod.shape
    tn = 1536
    c_pad = jnp.pad(c, ((0, SUBLANES - B), (0, 0)))
    out = pl.pallas_call(
        _mod_kernel,
        grid=(depth, n // tn),
        in_specs=[pl.BlockSpec((SUBLANES, D), lambda l, j: (0, 0)),
                  pl.BlockSpec((1, D, tn), lambda l, j: (l, 0, j)),
                  pl.BlockSpec((1, 1, tn), lambda l, j: (l, 0, j))],
        out_specs=pl.BlockSpec((1, SUBLANES, tn), lambda l, j: (l, 0, j)),
        out_shape=jax.ShapeDtypeStruct((depth, SUBLANES, n), F32),
        compiler_params=_params("parallel", "parallel"),
        name="mod",
    )(c_pad, w_mod, b_mod.reshape(depth, 1, n))
    return out[:, :B, :]


IN_PACKED = 2560


def _pack_w_in(w_in):
    D = w_in.shape[0]
    o_a = 4 * GDN_WIDTH
    o_dq = o_a + 2 * GDN_HEADS
    o_dkv = o_dq + MLA_Q_RANK
    o_kr = o_dkv + MLA_KV_RANK
    z = lambda n: jnp.zeros((D, n), w_in.dtype)
    return jnp.concatenate(
        [w_in[:, :o_a], w_in[:, o_dq:o_dkv], w_in[:, o_dkv:o_kr],
         w_in[:, o_a:o_dq], z(MLA_NOPE - 2 * GDN_HEADS), w_in[:, o_kr:o_kr + MLA_ROPE],
         z(LANES - MLA_NOPE - MLA_ROPE)], axis=1).astype(BF16)


def _inproj_kernel(x_ref, sh_ref, sc_ref, w_ref, qkvz_ref, lat_ref):
    h = _adaln(x_ref[0], sh_ref[0], sc_ref[0]).astype(BF16)
    y = jnp.dot(h, w_ref[...], preferred_element_type=F32)
    qkvz_ref[0] = y[:, :4 * GDN_WIDTH].astype(qkvz_ref.dtype)
    lat_ref[0] = y[:, 4 * GDN_WIDTH:]


def _inproj_call(x, shift, scale, w_packed):
    B, S, D = x.shape
    tm = min(512, S)
    n_lat = IN_PACKED - 4 * GDN_WIDTH
    return pl.pallas_call(
        _inproj_kernel,
        grid=(B, S // tm),
        in_specs=[pl.BlockSpec((1, tm, D), lambda b, i: (b, i, 0)),
                  pl.BlockSpec((1, 1, D), lambda b, i: (b, 0, 0)),
                  pl.BlockSpec((1, 1, D), lambda b, i: (b, 0, 0)),
                  pl.BlockSpec((D, IN_PACKED), lambda b, i: (0, 0))],
        out_specs=[pl.BlockSpec((1, tm, 4 * GDN_WIDTH), lambda b, i: (b, i, 0)),
                   pl.BlockSpec((1, tm, n_lat), lambda b, i: (b, i, 0))],
        out_shape=[jax.ShapeDtypeStruct((B, S, 4 * GDN_WIDTH), BF16),
                   jax.ShapeDtypeStruct((B, S, n_lat), F32)],
        compiler_params=_params("parallel", "parallel"),
        name="inproj",
    )(x, shift, scale, w_packed)


def _gdn_kernel(x_ref, ab_ref, cw_ref, alog_ref, dtb_ref, gain_ref, o_ref, ext_ref, st_ref):
    C, H, Dh, W = CHUNK, GDN_HEADS, GDN_HEAD_DIM, GDN_WIDTH

    @pl.when(pl.program_id(1) == 0)
    def _init():
        ext_ref[0:SUBLANES, :] = jnp.zeros((SUBLANES, 3 * W), F32)
        st_ref[...] = jnp.zeros_like(st_ref)

    R = x_ref.shape[1]
    x = x_ref[0].astype(F32)
    ext_ref[SUBLANES:SUBLANES + R, :] = x[:, :3 * W]
    base = SUBLANES - (GDN_CONV - 1)
    conv = cw_ref[0:1, :] * ext_ref[base:base + R, :]
    for j in range(1, GDN_CONV):
        conv = conv + cw_ref[j:j + 1, :] * ext_ref[base + j:base + j + R, :]
    ext_ref[0:SUBLANES, :] = x[R - SUBLANES:R, :3 * W]
    qkv = _silu(conv)

    W2 = 2 * Dh
    pairs = range(H // 2)
    causal, strict, eye_p, first = _pair_masks(C)
    tri = _tri_masks(C)[0].astype(BF16)
    r2 = lax.broadcasted_iota(jnp.int32, (W2, W2), 0)
    c2 = lax.broadcasted_iota(jnp.int32, (W2, W2), 1)
    same_head = (r2 // Dh) == (c2 // Dh)
    ones_bd = same_head.astype(BF16)
    eye2 = (r2 == c2).astype(BF16)
    first_row = first[0:1, :]
    ab = ab_ref[0]
    beta_all = jax.nn.sigmoid(ab)
    g_all = -jnp.exp(alog_ref[...]) * _softplus(ab + dtb_ref[...])
    qk_slabs = [qkv[:, off + p * W2:off + (p + 1) * W2] for off in (0, W) for p in pairs]
    qk_norm = [t * lax.rsqrt(ss + 1e-12)
               for t, ss in zip(qk_slabs, _head_sums([t * t for t in qk_slabs], ones_bd, split=True))]
    q_n, k_n = qk_norm[:len(pairs)], qk_norm[len(pairs):]

    units = []
    for c in range(R // C):
        rows = slice(c * C, (c + 1) * C)
        gcum = _tri_cumsum(tri, g_all[rows])
        gcum_t = gcum.T
        eg = jnp.exp(gcum)
        glast = gcum[C - 1:C, :]
        eg_rest = jnp.exp(glast - gcum)
        eg_last = jnp.exp(glast)
        for p in pairs:
            h0, h1 = 2 * p, 2 * p + 1
            per_head = lambda t, off=0: jnp.where(first, t[:, off + h0:off + h0 + 1], t[:, off + h1:off + h1 + 1])
            q = q_n[p][rows] * (Dh ** -0.5)
            k = k_n[p][rows]
            bc = per_head(beta_all[rows], H)
            eg_p = per_head(eg)
            kb = k * bc
            g_col = jnp.concatenate([gcum_t[h0:h0 + 1, :], gcum_t[h1:h1 + 1, :]], axis=1)
            decay = jnp.where(causal, jnp.exp(jnp.where(causal, per_head(gcum) - g_col, 0.0)), 0.0)
            units.append(dict(
                c=c, p=p, decay=decay, vb=qkv[rows, 2 * W + p * W2:2 * W + (p + 1) * W2] * bc, k=k,
                kbq=jnp.concatenate([kb, q], axis=0),
                kbq_g=jnp.concatenate([kb * eg_p, q * eg_p], axis=0),
                k_rest=k * per_head(eg_rest),
                e_last=jnp.where(first_row, eg_last[:, h0:h0 + 1], eg_last[:, h1:h1 + 1])))
    def independent(us):
        if not us:
            return
        for u in us:
            kk = _mm_nt(u["kbq"], _bd_rows(u["k"], first))
            u["L"] = jnp.where(strict, kk[:C] * u["decay"], 0.0)
            u["qk"] = jnp.where(causal, kk[C:] * u["decay"], 0.0)
            u["k_rest_t"] = _mm_nt(eye2, u["k_rest"])
        yield
        yield from _unit_lower_inv_pairs(us, eye_p, first)

    n_pairs = len(pairs)
    S = [st_ref[p] for p in pairs]

    def chain(chunks):
        for c in chunks:
            cu = units[c * n_pairs:(c + 1) * n_pairs]
            for u in cu:
                u["ks"] = _mm(u["kbq_g"], S[u["p"]])
            yield
            for u in cu:
                u["v_new"] = _mm(u["T"], _bd_rows(u["vb"] - u["ks"][:C], first))
            yield
            for u in cu:
                S[u["p"]] = (S[u["p"]] * u["e_last"]
                             + jnp.where(same_head, _mm(u["k_rest_t"], u["v_new"]), 0.0))
            yield

    n_chunks = R // C
    half = max(1, n_chunks // 2)
    _emit(independent(units[:half * n_pairs]))
    _emit(independent(units[half * n_pairs:]), chain(range(half)))
    _emit(chain(range(half, n_chunks)))
    for p in pairs:
        st_ref[p] = S[p]

    for u in units:
        u["o"] = u["ks"][C:] + _mm(u["qk"], _bd_rows(u["v_new"], first))
    for u, ss in zip(units, _head_sums([u["o"] * u["o"] for u in units], ones_bd)):
        rows = slice(u["c"] * C, (u["c"] + 1) * C)
        sl = slice(u["p"] * W2, (u["p"] + 1) * W2)
        o = u["o"] * lax.rsqrt(ss * (1.0 / Dh) + EPS)
        z = x[rows, 3 * W + u["p"] * W2:3 * W + (u["p"] + 1) * W2]
        o_ref[0, rows, sl] = (o * gain_ref[...] * _silu(z)).astype(o_ref.dtype)


def _gdn_call(qkvz, lat, conv_w, a_log, dt_bias, gain):
    B, S, _ = qkvz.shape
    H, Dh, W = GDN_HEADS, GDN_HEAD_DIM, GDN_WIDTH
    C = CHUNK * CHUNKS_PER_STEP
    lane_pad = lambda t: jnp.pad(t.reshape(1, H), ((0, 0), (0, LANES - H)))
    ab_block = (lat.shape[-1] // LANES) - 1
    return pl.pallas_call(
        _gdn_kernel,
        grid=(B, S // C),
        in_specs=[pl.BlockSpec((1, C, 4 * W), lambda b, c: (b, c, 0)),
                  pl.BlockSpec((1, C, LANES), lambda b, c: (b, c, ab_block)),
                  pl.BlockSpec((GDN_CONV, 3 * W), lambda b, c: (0, 0)),
                  pl.BlockSpec((1, LANES), lambda b, c: (0, 0)),
                  pl.BlockSpec((1, LANES), lambda b, c: (0, 0)),
                  pl.BlockSpec((1, 2 * Dh), lambda b, c: (0, 0))],
        out_specs=pl.BlockSpec((1, C, W), lambda b, c: (b, c, 0)),
        out_shape=jax.ShapeDtypeStruct((B, S, W), BF16),
        scratch_shapes=[pltpu.VMEM((C + SUBLANES, 3 * W), F32),
                        pltpu.VMEM((H // 2, 2 * Dh, 2 * Dh), F32)],
        compiler_params=_params("parallel", "arbitrary"),
        name="gdn",
    )(qkvz, lat, conv_w, lane_pad(a_log), lane_pad(dt_bias), jnp.tile(gain.reshape(1, Dh), (1, 2)))


def _mla_prep_kernel(lat_ref, pos_ref, invf_ref, qg_ref, kvg_ref, wuq_ref, wuk_ref, wuv_ref,
                     qhg_ref, khg_ref, q_ref, k_ref, v_ref):
    lat = lat_ref[0]
    dq = lat[:, :MLA_Q_RANK]
    dkv = lat[:, MLA_Q_RANK:MLA_Q_RANK + MLA_KV_RANK]
    slab = lat[:, MLA_Q_RANK + MLA_KV_RANK:]
    q_lat = dq * lax.rsqrt(jnp.mean(dq * dq, axis=-1, keepdims=True) + EPS) * qg_ref[...]
    kv_lat = dkv * lax.rsqrt(jnp.mean(dkv * dkv, axis=-1, keepdims=True) + EPS) * kvg_ref[...]
    qm = _mm(q_lat, wuq_ref[...])
    kn = _mm(kv_lat, wuk_ref[...])
    v_row = lax.broadcasted_iota(jnp.int32, (MLA_HEADS * LANES, 1), 0)
    v_ones = ((v_row % LANES) >= MLA_V).astype(F32)
    v_ref[0] = (_mm_nt(wuv_ref[...], kv_lat) + v_ones).astype(BF16)

    lane = lax.broadcasted_iota(jnp.int32, slab.shape, 1)
    half = MLA_ROPE // 2
    ang = pos_ref[0] * invf_ref[...]
    cosf = jnp.cos(ang)
    sinf = jnp.sin(ang)
    sin_signed = jnp.where(lane < MLA_NOPE + half, -sinf, sinf)
    k_rope = jnp.where((lane >= MLA_NOPE) & (lane < MLA_QK), slab, 0.0)

    def head_norm_rope(t, gain):
        t = t * lax.rsqrt(jnp.sum(t * t, axis=-1, keepdims=True) * (1.0 / MLA_QK) + EPS) * gain
        partner = jnp.where(lane < MLA_NOPE + half,
                            pltpu.roll(t, LANES - half, 1), pltpu.roll(t, half, 1))
        return t * cosf + partner * sin_signed

    scale = MLA_QK ** -0.5
    for h in range(MLA_HEADS):
        sl = slice(h * LANES, (h + 1) * LANES)
        q_ref[0, :, sl] = (head_norm_rope(qm[:, sl], qhg_ref[...]) * scale).astype(BF16)
        k_ref[0, :, sl] = head_norm_rope(kn[:, sl] + k_rope, khg_ref[...]).astype(BF16)


def _mla_prep_call(lat, positions, q_norm_gain, kv_norm_gain, w_uq, w_ukv, q_head_gain, k_head_gain):
    B, S, n_lat = lat.shape
    H = MLA_HEADS
    tm = min(512, S)
    wuq = jnp.pad(w_uq.reshape(MLA_Q_RANK, H, MLA_QK), ((0, 0), (0, 0), (0, LANES - MLA_QK)))
    wuq = wuq.reshape(MLA_Q_RANK, H * LANES).astype(BF16)
    wkv = w_ukv.reshape(MLA_KV_RANK, H, MLA_NOPE + MLA_V)
    wuk = jnp.pad(wkv[:, :, :MLA_NOPE], ((0, 0), (0, 0), (0, LANES - MLA_NOPE)))
    wuk = wuk.reshape(MLA_KV_RANK, H * LANES).astype(BF16)
    wuv = jnp.pad(wkv[:, :, MLA_NOPE:], ((0, 0), (0, 0), (0, LANES - MLA_V)))
    wuv = wuv.reshape(MLA_KV_RANK, H * LANES).T.astype(BF16)
    pad_gain = lambda g: jnp.pad(g.reshape(1, MLA_QK), ((0, 0), (0, LANES - MLA_QK)))
    inv_freq = ROPE_THETA ** (-jnp.arange(0, MLA_ROPE, 2, dtype=F32) / MLA_ROPE)
    invf = jnp.concatenate([jnp.zeros((MLA_NOPE,), F32), inv_freq, inv_freq,
                            jnp.zeros((LANES - MLA_QK,), F32)]).reshape(1, LANES)
    pos = positions.astype(F32).reshape(B, S, 1)
    const = lambda shape: pl.BlockSpec(shape, lambda b, i: (0,) * len(shape))
    return pl.pallas_call(
        _mla_prep_kernel,
        grid=(B, S // tm),
        in_specs=[pl.BlockSpec((1, tm, n_lat), lambda b, i: (b, i, 0)),
                  pl.BlockSpec((1, tm, 1), lambda b, i: (b, i, 0)),
                  const((1, LANES)), const((1, MLA_Q_RANK)), const((1, MLA_KV_RANK)),
                  const((MLA_Q_RANK, H * LANES)), const((MLA_KV_RANK, H * LANES)),
                  const((H * LANES, MLA_KV_RANK)), const((1, LANES)), const((1, LANES))],
        out_specs=[pl.BlockSpec((1, tm, H * LANES), lambda b, i: (b, i, 0)),
                   pl.BlockSpec((1, tm, H * LANES), lambda b, i: (b, i, 0)),
                   pl.BlockSpec((1, H * LANES, tm), lambda b, i: (b, 0, i))],
        out_shape=[jax.ShapeDtypeStruct((B, S, H * LANES), BF16)] * 2
        + [jax.ShapeDtypeStruct((B, H * LANES, S), BF16)],
        compiler_params=_params("parallel", "parallel"),
        name="mla_prep",
    )(lat, pos, invf, q_norm_gain.reshape(1, -1), kv_norm_gain.reshape(1, -1), wuq, wuk, wuv,
      pad_gain(q_head_gain), pad_gain(k_head_gain))


def _flash_kernel(q_ref, k_ref, vt_ref, o_ref, *, tq, tk):
    qi = pl.program_id(2)
    q = q_ref[0]
    q0, q1 = q[:, :LANES], q[:, LANES:]
    key = lax.broadcasted_iota(jnp.int32, (tk, tq), 0)
    qry = lax.broadcasted_iota(jnp.int32, (tk, tq), 1)

    def scores(kj):
        start = pl.multiple_of(kj * tk, tk)
        rows = pl.ds(start, tk)
        return (_mm_nt(k_ref[0, rows, :LANES], q0),
                _mm_nt(k_ref[0, rows, LANES:], q1))

    def online(st, m, acc, vt):
        m_new = jnp.maximum(m, jnp.max(st, axis=0, keepdims=True))
        pt = jnp.exp(st - m_new)
        return m_new, jnp.exp(m - m_new) * acc + _mm(vt, pt)

    def softmax_pv(kj, s0, s1, m0, m1, acc0, acc1):
        start = pl.multiple_of(kj * tk, tk)
        cols = pl.ds(start, tk)
        m0, acc0 = online(s0, m0, acc0, vt_ref[0, :LANES, cols])
        m1, acc1 = online(s1, m1, acc1, vt_ref[0, LANES:, cols])
        return m0, m1, acc0, acc1

    def group(t, carry):
        blocks = [FLASH_UNROLL * t + u for u in range(FLASH_UNROLL)]
        ss = [scores(kj) for kj in blocks]
        for kj, s in zip(blocks, ss):
            carry = softmax_pv(kj, *s, *carry)
        return carry

    def single(kj, carry):
        return softmax_pv(kj, *scores(kj), *carry)

    neg = jnp.full((1, tq), NEG_INF, F32)
    zero = jnp.zeros((LANES, tq), F32)
    n_full = qi * (tq // tk)
    n_group = n_full // FLASH_UNROLL
    carry = lax.fori_loop(0, n_group, group, (neg, neg, zero, zero))
    carry = lax.fori_loop(n_group * FLASH_UNROLL, n_full, single, carry)
    for d in range(tq // tk):
        s0, s1 = scores(n_full + d)
        mask = key + d * tk <= qry
        carry = softmax_pv(n_full + d, jnp.where(mask, s0, NEG_INF), jnp.where(mask, s1, NEG_INF), *carry)
    _, _, acc0, acc1 = carry
    out_t = jnp.concatenate([acc0[:MLA_V] / acc0[MLA_V:MLA_V + 1],
                             acc1[:MLA_V] / acc1[MLA_V:MLA_V + 1]], axis=0)
    o_ref[0] = out_t.T.astype(o_ref.dtype)


def _flash_call(q, k, v):
    B, S, _ = q.shape
    tq = min(512, S)
    pairs = MLA_HEADS // 2
    return pl.pallas_call(
        functools.partial(_flash_kernel, tq=tq, tk=min(FLASH_TK, tq)),
        grid=(B, pairs, S // tq),
        in_specs=[pl.BlockSpec((1, tq, 2 * LANES), lambda b, p, i: (b, i, p)),
                  pl.BlockSpec((1, S, 2 * LANES), lambda b, p, i: (b, 0, p)),
                  pl.BlockSpec((1, 2 * LANES, S), lambda b, p, i: (b, p, 0))],
        out_specs=pl.BlockSpec((1, tq, 2 * MLA_V), lambda b, p, i: (b, i, p)),
        out_shape=jax.ShapeDtypeStruct((B, S, MLA_HEADS * MLA_V), BF16),
        compiler_params=_params("parallel", "parallel", "arbitrary"),
        name="flash",
    )(q, k, v)


def _outproj_kernel(*refs, n_in):
    ins, (x_ref, gate_ref, w_ref, o_ref) = refs[:n_in], refs[n_in:]
    y = None
    off = 0
    for r in ins:
        n = r.shape[-1]
        part = _mm(r[0], w_ref[off:off + n, :])
        y = part if y is None else y + part
        off += n
    o_ref[0] = x_ref[0] + gate_ref[0] * y


def _outproj_call(ins, x, gate, w):
    B, S, D = x.shape
    tm = min(512, S)
    n_in = len(ins)
    return pl.pallas_call(
        functools.partial(_outproj_kernel, n_in=n_in),
        grid=(B, S // tm),
        in_specs=[pl.BlockSpec((1, tm, a.shape[-1]), lambda b, i: (b, i, 0)) for a in ins]
        + [pl.BlockSpec((1, tm, D), lambda b, i: (b, i, 0)),
           pl.BlockSpec((1, 1, D), lambda b, i: (b, 0, 0)),
           pl.BlockSpec(w.shape, lambda b, i: (0, 0))],
        out_specs=pl.BlockSpec((1, tm, D), lambda b, i: (b, i, 0)),
        out_shape=jax.ShapeDtypeStruct((B, S, D), F32),
        compiler_params=_params("parallel", "parallel"),
        name="outproj",
    )(*ins, x, gate, w)


def _mlp_kernel(x_ref, sh_ref, sc_ref, gate_ref, wu_ref, wd_ref, o_ref, h_ref, acc_ref):
    f = pl.program_id(2)

    @pl.when(f == 0)
    def _first():
        h_ref[...] = _adaln(x_ref[0], sh_ref[0], sc_ref[0]).astype(BF16)
        acc_ref[...] = jnp.zeros_like(acc_ref)

    u = jnp.maximum(jnp.dot(h_ref[...], wu_ref[...], preferred_element_type=F32), 0.0)
    acc_ref[...] += _mm(u * u, wd_ref[...])

    @pl.when(f == pl.num_programs(2) - 1)
    def _last():
        o_ref[0] = x_ref[0] + gate_ref[0] * acc_ref[...]


def _mlp_call(x, shift, scale, gate, w_up, w_down):
    B, S, D = x.shape
    F = w_up.shape[1]
    tm = min(1024, S)
    tf = 1024
    vec = pl.BlockSpec((1, 1, D), lambda b, i, f: (b, 0, 0))
    return pl.pallas_call(
        _mlp_kernel,
        grid=(B, S // tm, F // tf),
        in_specs=[pl.BlockSpec((1, tm, D), lambda b, i, f: (b, i, 0)), vec, vec, vec,
                  pl.BlockSpec((D, tf), lambda b, i, f: (0, f)),
                  pl.BlockSpec((tf, D), lambda b, i, f: (f, 0))],
        out_specs=pl.BlockSpec((1, tm, D), lambda b, i, f: (b, i, 0)),
        out_shape=jax.ShapeDtypeStruct((B, S, D), F32),
        scratch_shapes=[pltpu.VMEM((tm, D), BF16), pltpu.VMEM((tm, D), F32)],
        compiler_params=_params("parallel", "parallel", "arbitrary"),
        name="mlp",
    )(x, shift, scale, gate, w_up, w_down)


def _rwkv_proj_kernel(x_ref, xp_ref, sh_ref, sc_ref, mu_ref, wr_ref, wk_ref, wv_ref,
                      w1_ref, w2_ref, a1_ref, a2_ref, g1_ref, g2_ref, w0_ref, a0_ref,
                      r_ref, w_ref, k_ref, v_ref, a_ref, g_ref):
    h = _adaln(x_ref[0], sh_ref[0], sc_ref[0])
    tm = h.shape[0]
    prev_tail = _adaln(xp_ref[0], sh_ref[0], sc_ref[0])[SUBLANES - 1:SUBLANES, :]
    prev_tail = jnp.where(pl.program_id(1) == 0, 0.0, prev_tail)
    row = lax.broadcasted_iota(jnp.int32, h.shape, 0)
    h_prev = jnp.where(row == 0, prev_tail, pltpu.roll(h, 1, 0))
    xx = h_prev - h
    mix = lambda j: h + xx * mu_ref[j:j + 1, :]
    xr, xw, xk, xv, xa, xg = (mix(j) for j in range(6))
    r_ref[0] = _mm(xr, wr_ref[...]).astype(r_ref.dtype)
    k_ref[0] = _mm(xk, wk_ref[...]).astype(k_ref.dtype)
    v_ref[0] = _mm(xv, wv_ref[...]).astype(v_ref.dtype)
    wl = w0_ref[...] + _mm(jnp.tanh(_mm(xw, w1_ref[...])), w2_ref[...])
    w_log = -_softplus(-wl) - 0.5
    w_ref[0] = -jnp.exp(w_log)
    a_ref[0] = jax.nn.sigmoid(a0_ref[...] + _mm(_mm(xa, a1_ref[...]), a2_ref[...])).astype(a_ref.dtype)
    g_ref[0] = _mm(jax.nn.sigmoid(_mm(xg, g1_ref[...])), g2_ref[...]).astype(g_ref.dtype)


def _pad_lora(w_a, w_b):
    r = w_a.shape[1]
    rp = -(-r // LANES) * LANES
    return (jnp.pad(w_a, ((0, 0), (0, rp - r))).astype(BF16),
            jnp.pad(w_b, ((0, rp - r), (0, 0))).astype(BF16))


def _rwkv_proj_call(x, shift, scale, mu, w_r, w_k, w_v, w1, w2, a1, a2, g1, g2, w0, a0):
    B, S, D = x.shape
    tm = min(256, S)
    w1p, w2p = _pad_lora(w1, w2)
    a1p, a2p = _pad_lora(a1, a2)
    g1p, g2p = _pad_lora(g1, g2)
    const = lambda t: pl.BlockSpec(t.shape, lambda b, i: (0,) * t.ndim)
    vec = pl.BlockSpec((1, 1, D), lambda b, i: (b, 0, 0))
    row_blocks = tm // SUBLANES
    weights = (mu, w_r.astype(BF16), w_k.astype(BF16), w_v.astype(BF16), w1p, w2p, a1p, a2p,
               g1p, g2p, w0.reshape(1, D), a0.reshape(1, D))
    out = pl.BlockSpec((1, tm, D), lambda b, i: (b, i, 0))
    return pl.pallas_call(
        _rwkv_proj_kernel,
        grid=(B, S // tm),
        in_specs=[pl.BlockSpec((1, tm, D), lambda b, i: (b, i, 0)),
                  pl.BlockSpec((1, SUBLANES, D),
                               lambda b, i: (b, jnp.maximum(i * row_blocks - 1, 0), 0)),
                  vec, vec] + [const(t) for t in weights],
        out_specs=[out] * 6,
        out_shape=[jax.ShapeDtypeStruct((B, S, D), F32 if i == 1 else BF16) for i in range(6)],
        compiler_params=_params("parallel", "parallel"),
        name="rwkv_proj",
    )(x, x, shift, scale, *weights)


def _rwkv_kernel(r_ref, w_ref, k_ref, v_ref, a_ref, g_ref, kk_ref, ka_ref, rk_ref, lng_ref, lnb_ref,
                 o_ref, st_ref):
    C, Dh = CHUNK, RWKV_HEAD_DIM
    W2 = 2 * Dh
    pairs = range(RWKV_HEADS // 2)

    @pl.when(pl.program_id(1) == 0)
    def _init():
        st_ref[...] = jnp.zeros_like(st_ref)

    R = r_ref.shape[1]
    incl, strict, eye_p, first = _pair_masks(C)
    tri = _tri_masks(C)[0].astype(BF16)
    r2 = lax.broadcasted_iota(jnp.int32, (W2, W2), 0)
    c2 = lax.broadcasted_iota(jnp.int32, (W2, W2), 1)
    same_head = (r2 // Dh) == (c2 // Dh)
    ones_bd = same_head.astype(BF16)
    eye2 = (r2 == c2).astype(BF16)
    inv_dh = 1.0 / Dh

    units = []
    for c in range(R // C):
        rows = slice(c * C, (c + 1) * C)
        w = w_ref[0, rows, :]
        G = _tri_cumsum(tri, w)
        g_last = G[C - 1:C, :]
        e_in = jnp.exp(G)
        e_ex = jnp.exp(G - w)
        e_neg = jnp.exp(-G)
        e_rest = jnp.exp(g_last - G)
        e_last_col = jnp.broadcast_to(jnp.exp(g_last), (SUBLANES, g_last.shape[1])).T
        r = r_ref[0, rows, :].astype(F32)
        k0 = k_ref[0, rows, :].astype(F32)
        a = a_ref[0, rows, :].astype(F32)
        kk_raw = k0 * kk_ref[...]
        k_mod = k0 * (1.0 + (a - 1.0) * ka_ref[...])
        kk_slabs = [kk_raw[:, p * W2:(p + 1) * W2] for p in pairs]
        kk_sums = _head_sums([t * t for t in kk_slabs], ones_bd, split=True)
        for p in pairs:
            sl = slice(p * W2, (p + 1) * W2)
            kk = kk_slabs[p] * lax.rsqrt(kk_sums[p] + 1e-12)
            kh, rh, vh = k_mod[:, sl], r[:, sl], v_ref[0, rows, sl].astype(F32)
            bv = kk * a[:, sl]
            units.append(dict(
                c=c, sl=sl, rh=rh, kh=kh, vh=vh,
                ar=jnp.concatenate([-kk * e_ex[:, sl], rh * e_in[:, sl]], axis=0),
                b_t=bv * e_neg[:, sl], k_t=kh * e_neg[:, sl],
                bk_rest=jnp.concatenate([bv * e_rest[:, sl], kh * e_rest[:, sl]], axis=0),
                e_last=e_last_col[sl, 0:1]))
    def independent(us):
        if not us:
            return
        for u in us:
            ab = _mm_nt(u["ar"], _bd_rows(u["b_t"], first))
            ak = _mm_nt(u["ar"], _bd_rows(u["k_t"], first))
            u["L"] = -jnp.where(strict, ab[:C], 0.0)
            u["a_rb"] = jnp.where(incl, ab[C:], 0.0)
            u["akrk"] = jnp.concatenate([jnp.where(strict, ak[:C], 0.0), jnp.where(incl, ak[C:], 0.0)],
                                        axis=0)
            u["bk_rest_t"] = _mm_nt(eye2, u["bk_rest"])
        yield
        for u in us:
            u["kv"] = _mm(u["akrk"], _bd_rows(u["vh"], first))
        yield
        yield from _unit_lower_inv_pairs(us, eye_p, first)

    n_pairs = len(pairs)
    S = [st_ref[p] for p in pairs]

    def chain(chunks):
        for c in chunks:
            cu = units[c * n_pairs:(c + 1) * n_pairs]
            for p, u in zip(pairs, cu):
                u["sd"] = _mm(u["ar"], S[p]) + u["kv"]
            yield
            for u in cu:
                u["P"] = _mm(u["T"], _bd_rows(u["sd"][:C], first))
            yield
            for p, u in zip(pairs, cu):
                pv = jnp.concatenate([u["P"], u["vh"]], axis=0)
                S[p] = S[p] * u["e_last"] + jnp.where(same_head, _mm(u["bk_rest_t"], pv), 0.0)
            yield

    n_chunks = R // C
    half = max(1, n_chunks // 2)
    _emit(independent(units[:half * n_pairs]))
    _emit(independent(units[half * n_pairs:]), chain(range(half)))
    _emit(chain(range(half, n_chunks)))
    for p in pairs:
        st_ref[p] = S[p]

    for u in units:
        u["y"] = u["sd"][C:] + _mm(u["a_rb"], _bd_rows(u["P"], first))
    sums = _head_sums([u["y"] for u in units] + [u["rh"] * u["kh"] * rk_ref[:, u["sl"]] for u in units],
                      ones_bd)
    for u, s_y, s_rk in zip(units, sums[:len(units)], sums[len(units):]):
        u["yc"] = u["y"] - s_y * inv_dh
        u["bonus"] = s_rk * u["vh"]
    for u, s_var in zip(units, _head_sums([u["yc"] * u["yc"] for u in units], ones_bd)):
        sl = u["sl"]
        rows = slice(u["c"] * C, (u["c"] + 1) * C)
        yn = u["yc"] * lax.rsqrt(s_var * inv_dh + RWKV_GN_EPS) * lng_ref[:, sl] + lnb_ref[:, sl]
        o_ref[0, rows, sl] = ((yn + u["bonus"]) * g_ref[0, rows, sl].astype(F32)).astype(o_ref.dtype)


def _rwkv_call(r, w, k, v, a, g, k_k, k_a, r_k, ln_gain, ln_bias):
    B, S, D = r.shape
    H, Dh = RWKV_HEADS, RWKV_HEAD_DIM
    C = CHUNK * CHUNKS_PER_STEP
    seq = pl.BlockSpec((1, C, D), lambda b, c: (b, c, 0))
    vec = pl.BlockSpec((1, D), lambda b, c: (0, 0))
    flat = lambda t: t.reshape(1, D)
    return pl.pallas_call(
        _rwkv_kernel,
        grid=(B, S // C),
        in_specs=[seq] * 6 + [vec] * 5,
        out_specs=seq,
        out_shape=jax.ShapeDtypeStruct((B, S, D), BF16),
        scratch_shapes=[pltpu.VMEM((H // 2, 2 * Dh, 2 * Dh), F32)],
        compiler_params=_params("parallel", "arbitrary"),
        name="rwkv",
    )(r, w, k, v, a, g, flat(k_k), flat(k_a), flat(r_k), flat(ln_gain), flat(ln_bias))


def kernel(x, c, positions, w_mod, b_mod, w_in0, gdn_conv_w, gdn_a_log, gdn_dt_bias, gdn_norm_gain, mla_q_norm_gain, mla_kv_norm_gain, mla_w_uq, mla_w_ukv, mla_q_head_gain, mla_k_head_gain, w_out0, rwkv_mu, rwkv_w_r, rwkv_w_k, rwkv_w_v, rwkv_w_o, rwkv_w0, rwkv_w1, rwkv_w2, rwkv_a0, rwkv_a1, rwkv_a2, rwkv_g1, rwkv_g2, rwkv_k_k, rwkv_k_a, rwkv_r_k, rwkv_ln_gain, rwkv_ln_bias, w_up, w_down):
    B, S, D = x.shape
    mod = _mod_call(c, w_mod, b_mod)

    def mod_chunks(layer):
        return [mod[layer, :, i * D:(i + 1) * D].reshape(B, 1, D) for i in range(6)]

    shift1, scale1, gate1, shift2, scale2, gate2 = mod_chunks(0)
    qkvz, lat = _inproj_call(x, shift1, scale1, _pack_w_in(w_in0[0]))
    o_gdn = _gdn_call(qkvz, lat, gdn_conv_w[0], gdn_a_log[0], gdn_dt_bias[0], gdn_norm_gain[0])
    q, k, v = _mla_prep_call(lat, positions, mla_q_norm_gain[0], mla_kv_norm_gain[0], mla_w_uq[0],
                             mla_w_ukv[0], mla_q_head_gain[0], mla_k_head_gain[0])
    o_mla = _flash_call(q, k, v)
    x = _outproj_call([o_gdn, o_mla], x, gate1, w_out0[0].astype(BF16))
    x = _mlp_call(x, shift2, scale2, gate2, w_up[0].astype(BF16), w_down[0].astype(BF16))

    shift1, scale1, gate1, shift2, scale2, gate2 = mod_chunks(1)
    r, w, k, v, a, g = _rwkv_proj_call(x, shift1, scale1, rwkv_mu[0], rwkv_w_r[0], rwkv_w_k[0],
                                       rwkv_w_v[0], rwkv_w1[0], rwkv_w2[0], rwkv_a1[0], rwkv_a2[0],
                                       rwkv_g1[0], rwkv_g2[0], rwkv_w0[0], rwkv_a0[0])
    y = _rwkv_call(r, w, k, v, a, g, rwkv_k_k[0], rwkv_k_a[0], rwkv_r_k[0], rwkv_ln_gain[0],
                   rwkv_ln_bias[0])
    x = _outproj_call([y], x, gate1, rwkv_w_o[0].astype(BF16))
    x = _mlp_call(x, shift2, scale2, gate2, w_up[1].astype(BF16), w_down[1].astype(BF16))
    return x
```

```python
import functools
import math

import jax
import jax.numpy as jnp
from jax import lax
from jax.experimental import pallas as pl
from jax.experimental.pallas import tpu as pltpu

F32 = jnp.float32
BF16 = jnp.bfloat16

D_MODEL = 1024
DEPTH = 2
GDN_HEADS = 8
GDN_HEAD_DIM = 64
GDN_WIDTH = GDN_HEADS * GDN_HEAD_DIM
GDN_CONV = 4
CHUNK = 64
CHUNKS_PER_STEP = 4
FLASH_TQ = 512
FLASH_TK = 512
FLASH_UNROLL = 2
MLA_HEADS = 8
MLA_Q_RANK = 256
MLA_KV_RANK = 128
MLA_NOPE = 64
MLA_ROPE = 32
MLA_QK = MLA_NOPE + MLA_ROPE
MLA_V = 64
ROPE_THETA = 10000.0
NEG_INF = -1e30
RWKV_HEADS = 16
RWKV_HEAD_DIM = 64
RWKV_GN_EPS = 64e-5
D_FF = 4 * D_MODEL
EPS = 1e-6
LANES = 128
SUBLANES = 8
VMEM_LIMIT = 56 * 1024 * 1024


def _params(*sem):
    return pltpu.CompilerParams(dimension_semantics=sem, vmem_limit_bytes=VMEM_LIMIT)


def _mm(a, b):
    return jnp.dot(a.astype(BF16), b.astype(BF16), preferred_element_type=F32)


def _mm_nt(a, b):
    return lax.dot_general(a.astype(BF16), b.astype(BF16), (((1,), (1,)), ((), ())),
                           preferred_element_type=F32)


def _tri_cumsum(tri_bf16, x):
    hi = x.astype(BF16)
    r1 = x - hi.astype(F32)
    mid = r1.astype(BF16)
    lo = (r1 - mid.astype(F32)).astype(BF16)
    dot = functools.partial(jnp.dot, preferred_element_type=F32)
    return dot(tri_bf16, hi) + dot(tri_bf16, mid) + dot(tri_bf16, lo)


def _unit_lower_inv(Ls, eye):
    n = Ls[0].shape[0]
    row = lax.broadcasted_iota(jnp.int32, (n, n), 0)
    col = lax.broadcasted_iota(jnp.int32, (n, n), 1)
    Ts = [eye] * len(Ls)
    s = 1
    while s < n:
        lower_left = ((row // (2 * s)) == (col // (2 * s))) & ((row // s) % 2 == 1) & ((col // s) % 2 == 0)
        Cs = [jnp.where(lower_left, L, 0.0) for L in Ls]
        if s == 1:
            Ts = [T - C for T, C in zip(Ts, Cs)]
        else:
            Xs = [_mm(T, C) for T, C in zip(Ts, Cs)]
            Ts = [T - _mm(X, T) for T, X in zip(Ts, Xs)]
        s *= 2
    return Ts


def _pair_masks(n):
    row = lax.broadcasted_iota(jnp.int32, (n, 2 * n), 0)
    lane = lax.broadcasted_iota(jnp.int32, (n, 2 * n), 1)
    col = lane % n
    return row >= col, row > col, (row == col).astype(F32), lane < n


def _bd_rows(x, first):
    x = x.astype(BF16)
    zero = jnp.zeros_like(x)
    return jnp.concatenate([jnp.where(first, x, zero), jnp.where(first, zero, x)], axis=0)


def _unit_lower_inv_pairs(units, eye_p, first):
    n = units[0]["L"].shape[0]
    row = lax.broadcasted_iota(jnp.int32, (n, 2 * n), 0)
    col = lax.broadcasted_iota(jnp.int32, (n, 2 * n), 1) % n
    Ts = [eye_p] * len(units)
    s = 1
    while s < n:
        lower_left = ((row // (2 * s)) == (col // (2 * s))) & ((row // s) % 2 == 1) & ((col // s) % 2 == 0)
        Cs = [jnp.where(lower_left, u["L"], 0.0) for u in units]
        if s == 1:
            Ts = [T - C for T, C in zip(Ts, Cs)]
        else:
            Xs = [_mm(T, _bd_rows(C, first)) for T, C in zip(Ts, Cs)]
            yield
            Ts = [T - _mm(X, _bd_rows(T, first)) for T, X in zip(Ts, Xs)]
            yield
        s *= 2
    for u, T in zip(units, Ts):
        u["T"] = T


def _emit(*gens):
    live = list(gens)
    while live:
        for g in list(live):
            try:
                next(g)
            except StopIteration:
                live.remove(g)


def _head_sums(xs, ones_bd, split=False):
    n = xs[0].shape[0]
    x = jnp.concatenate(xs, axis=0)
    hi = x.astype(BF16)
    out = jnp.dot(hi, ones_bd, preferred_element_type=F32)
    if split:
        out = out + jnp.dot((x - hi.astype(F32)).astype(BF16), ones_bd, preferred_element_type=F32)
    return [out[i * n:(i + 1) * n] for i in range(len(xs))]


def _softplus(x):
    return jnp.maximum(x, 0.0) + jnp.log1p(jnp.exp(-jnp.abs(x)))


def _silu(x):
    return x * jax.nn.sigmoid(x)


def _adaln(x, shift, scale):
    ms = jnp.mean(x * x, axis=-1, keepdims=True)
    return x * lax.rsqrt(ms + EPS) * (1.0 + scale) + shift


def _tri_masks(n):
    row = lax.broadcasted_iota(jnp.int32, (n, n), 0)
    col = lax.broadcasted_iota(jnp.int32, (n, n), 1)
    return row >= col, row > col, (row == col).astype(F32)


def _mod_kernel(c_ref, w_ref, b_ref, o_ref):
    cond = _silu(c_ref[...])
    o_ref[0] = jnp.dot(cond, w_ref[0], precision=lax.Precision.HIGHEST,
                       preferred_element_type=F32) + b_ref[0]


def _mod_call(c, w_mod, b_mod):
    B, D = c.shape
    depth, _, n = w_mod.shape
    tn = 1536
    c_pad = jnp.pad(c, ((0, SUBLANES - B), (0, 0)))
    out = pl.pallas_call(
        _mod_kernel,
        grid=(depth, n // tn),
        in_specs=[pl.BlockSpec((SUBLANES, D), lambda l, j: (0, 0)),
                  pl.BlockSpec((1, D, tn), lambda l, j: (l, 0, j)),
                  pl.BlockSpec((1, 1, tn), lambda l, j: (l, 0, j))],
        out_specs=pl.BlockSpec((1, SUBLANES, tn), lambda l, j: (l, 0, j)),
        out_shape=jax.ShapeDtypeStruct((depth, SUBLANES, n), F32),
        compiler_params=_params("parallel", "parallel"),
        name="mod",
    )(c_pad, w_mod, b_mod.reshape(depth, 1, n))
    return out[:, :B, :]


IN_PACKED = 2560


def _pack_w_in(w_in):
    D = w_in.shape[0]
    o_a = 4 * GDN_WIDTH
    o_dq = o_a + 2 * GDN_HEADS
    o_dkv = o_dq + MLA_Q_RANK
    o_kr = o_dkv + MLA_KV_RANK
    z = lambda n: jnp.zeros((D, n), w_in.dtype)
    return jnp.concatenate(
        [w_in[:, :o_a], w_in[:, o_dq:o_dkv], w_in[:, o_dkv:o_kr],
         w_in[:, o_a:o_dq], z(MLA_NOPE - 2 * GDN_HEADS), w_in[:, o_kr:o_kr + MLA_ROPE],
         z(LANES - MLA_NOPE - MLA_ROPE)], axis=1).astype(BF16)


def _inproj_kernel(x_ref, sh_ref, sc_ref, w_ref, qkvz_ref, lat_ref):
    h = _adaln(x_ref[0], sh_ref[0], sc_ref[0]).astype(BF16)
    y = jnp.dot(h, w_ref[...], preferred_element_type=F32)
    qkvz_ref[0] = y[:, :4 * GDN_WIDTH].astype(qkvz_ref.dtype)
    lat_ref[0] = y[:, 4 * GDN_WIDTH:]


def _inproj_call(x, shift, scale, w_packed):
    B, S, D = x.shape
    tm = min(1024, S)
    n_lat = IN_PACKED - 4 * GDN_WIDTH
    return pl.pallas_call(
        _inproj_kernel,
        grid=(B, S // tm),
        in_specs=[pl.BlockSpec((1, tm, D), lambda b, i: (b, i, 0)),
                  pl.BlockSpec((1, 1, D), lambda b, i: (b, 0, 0)),
                  pl.BlockSpec((1, 1, D), lambda b, i: (b, 0, 0)),
                  pl.BlockSpec((D, IN_PACKED), lambda b, i: (0, 0))],
        out_specs=[pl.BlockSpec((1, tm, 4 * GDN_WIDTH), lambda b, i: (b, i, 0)),
                   pl.BlockSpec((1, tm, n_lat), lambda b, i: (b, i, 0))],
        out_shape=[jax.ShapeDtypeStruct((B, S, 4 * GDN_WIDTH), BF16),
                   jax.ShapeDtypeStruct((B, S, n_lat), F32)],
        compiler_params=_params("parallel", "parallel"),
        name="inproj",
    )(x, shift, scale, w_packed)


def _gdn_kernel(x_ref, ab_ref, cw_ref, alog_ref, dtb_ref, gain_ref, o_ref, ext_ref, st_ref):
    C, H, Dh, W = CHUNK, GDN_HEADS, GDN_HEAD_DIM, GDN_WIDTH

    @pl.when(pl.program_id(1) == 0)
    def _init():
        ext_ref[0:SUBLANES, :] = jnp.zeros((SUBLANES, 3 * W), F32)
        st_ref[...] = jnp.zeros_like(st_ref)

    R = x_ref.shape[1]
    x = x_ref[0].astype(F32)
    ext_ref[SUBLANES:SUBLANES + R, :] = x[:, :3 * W]
    base = SUBLANES - (GDN_CONV - 1)
    conv = cw_ref[0:1, :] * ext_ref[base:base + R, :]
    for j in range(1, GDN_CONV):
        conv = conv + cw_ref[j:j + 1, :] * ext_ref[base + j:base + j + R, :]
    ext_ref[0:SUBLANES, :] = x[R - SUBLANES:R, :3 * W]
    qkv = _silu(conv)

    W2 = 2 * Dh
    pairs = range(H // 2)
    causal, strict, eye_p, first = _pair_masks(C)
    tri = _tri_masks(C)[0].astype(BF16)
    r2 = lax.broadcasted_iota(jnp.int32, (W2, W2), 0)
    c2 = lax.broadcasted_iota(jnp.int32, (W2, W2), 1)
    same_head = (r2 // Dh) == (c2 // Dh)
    ones_bd = same_head.astype(BF16)
    eye2 = (r2 == c2).astype(BF16)
    first_row = first[0:1, :]
    ab = ab_ref[0]
    beta_all = jax.nn.sigmoid(ab)
    g_all = -jnp.exp(alog_ref[...]) * _softplus(ab + dtb_ref[...])
    qk_slabs = [qkv[:, off + p * W2:off + (p + 1) * W2] for off in (0, W) for p in pairs]
    qk_norm = [t * lax.rsqrt(ss + 1e-12)
               for t, ss in zip(qk_slabs, _head_sums([t * t for t in qk_slabs], ones_bd, split=True))]
    q_n, k_n = qk_norm[:len(pairs)], qk_norm[len(pairs):]

    units = []
    for c in range(R // C):
        rows = slice(c * C, (c + 1) * C)
        gcum = _tri_cumsum(tri, g_all[rows])
        gcum_t = gcum.T
        eg = jnp.exp(gcum)
        glast = gcum[C - 1:C, :]
        eg_rest = jnp.exp(glast - gcum)
        eg_last = jnp.exp(glast)
        for p in pairs:
            h0, h1 = 2 * p, 2 * p + 1
            per_head = lambda t, off=0: jnp.where(first, t[:, off + h0:off + h0 + 1], t[:, off + h1:off + h1 + 1])
            q = q_n[p][rows] * (Dh ** -0.5)
            k = k_n[p][rows]
            bc = per_head(beta_all[rows], H)
            eg_p = per_head(eg)
            kb = k * bc
            g_col = jnp.concatenate([gcum_t[h0:h0 + 1, :], gcum_t[h1:h1 + 1, :]], axis=1)
            decay = jnp.where(causal, jnp.exp(jnp.where(causal, per_head(gcum) - g_col, 0.0)), 0.0)
            units.append(dict(
                c=c, p=p, decay=decay, vb=qkv[rows, 2 * W + p * W2:2 * W + (p + 1) * W2] * bc, k=k,
                kbq=jnp.concatenate([kb, q], axis=0),
                kbq_g=jnp.concatenate([kb * eg_p, q * eg_p], axis=0),
                k_rest=k * per_head(eg_rest),
                e_last=jnp.where(first_row, eg_last[:, h0:h0 + 1], eg_last[:, h1:h1 + 1])))
    def independent(us):
        if not us:
            return
        for u in us:
            kk = _mm_nt(u["kbq"], _bd_rows(u["k"], first))
            u["L"] = jnp.where(strict, kk[:C] * u["decay"], 0.0)
            u["qk"] = jnp.where(causal, kk[C:] * u["decay"], 0.0)
            u["k_rest_t"] = _mm_nt(eye2, u["k_rest"])
        yield
        yield from _unit_lower_inv_pairs(us, eye_p, first)

    n_pairs = len(pairs)
    S = [st_ref[p] for p in pairs]

    def chain(chunks):
        for c in chunks:
            cu = units[c * n_pairs:(c + 1) * n_pairs]
            for u in cu:
                u["ks"] = _mm(u["kbq_g"], S[u["p"]])
            yield
            for u in cu:
                u["v_new"] = _mm(u["T"], _bd_rows(u["vb"] - u["ks"][:C], first))
            yield
            for u in cu:
                S[u["p"]] = (S[u["p"]] * u["e_last"]
                             + jnp.where(same_head, _mm(u["k_rest_t"], u["v_new"]), 0.0))
            yield

    n_chunks = R // C
    half = max(1, n_chunks // 2)
    _emit(independent(units[:half * n_pairs]))
    _emit(independent(units[half * n_pairs:]), chain(range(half)))
    _emit(chain(range(half, n_chunks)))
    for p in pairs:
        st_ref[p] = S[p]

    for u in units:
        u["o"] = u["ks"][C:] + _mm(u["qk"], _bd_rows(u["v_new"], first))
    for u, ss in zip(units, _head_sums([u["o"] * u["o"] for u in units], ones_bd)):
        rows = slice(u["c"] * C, (u["c"] + 1) * C)
        sl = slice(u["p"] * W2, (u["p"] + 1) * W2)
        o = u["o"] * lax.rsqrt(ss * (1.0 / Dh) + EPS)
        z = x[rows, 3 * W + u["p"] * W2:3 * W + (u["p"] + 1) * W2]
        o_ref[0, rows, sl] = (o * gain_ref[...] * _silu(z)).astype(o_ref.dtype)


def _gdn_call(qkvz, lat, conv_w, a_log, dt_bias, gain):
    B, S, _ = qkvz.shape
    H, Dh, W = GDN_HEADS, GDN_HEAD_DIM, GDN_WIDTH
    C = CHUNK * CHUNKS_PER_STEP
    lane_pad = lambda t: jnp.pad(t.reshape(1, H), ((0, 0), (0, LANES - H)))
    ab_block = (lat.shape[-1] // LANES) - 1
    return pl.pallas_call(
        _gdn_kernel,
        grid=(B, S // C),
        in_specs=[pl.BlockSpec((1, C, 4 * W), lambda b, c: (b, c, 0)),
                  pl.BlockSpec((1, C, LANES), lambda b, c: (b, c, ab_block)),
                  pl.BlockSpec((GDN_CONV, 3 * W), lambda b, c: (0, 0)),
                  pl.BlockSpec((1, LANES), lambda b, c: (0, 0)),
                  pl.BlockSpec((1, LANES), lambda b, c: (0, 0)),
                  pl.BlockSpec((1, 2 * Dh), lambda b, c: (0, 0))],
        out_specs=pl.BlockSpec((1, C, W), lambda b, c: (b, c, 0)),
        out_shape=jax.ShapeDtypeStruct((B, S, W), BF16),
        scratch_shapes=[pltpu.VMEM((C + SUBLANES, 3 * W), F32),
                        pltpu.VMEM((H // 2, 2 * Dh, 2 * Dh), F32)],
        compiler_params=_params("parallel", "arbitrary"),
        name="gdn",
    )(qkvz, lat, conv_w, lane_pad(a_log), lane_pad(dt_bias), jnp.tile(gain.reshape(1, Dh), (1, 2)))


def _mla_prep_kernel(lat_ref, pos_ref, invf_ref, qg_ref, kvg_ref, wuq_ref, wuk_ref, wuv_ref,
                     qhg_ref, khg_ref, q_ref, k_ref, v_ref):
    lat = lat_ref[0]
    dq = lat[:, :MLA_Q_RANK]
    dkv = lat[:, MLA_Q_RANK:MLA_Q_RANK + MLA_KV_RANK]
    slab = lat[:, MLA_Q_RANK + MLA_KV_RANK:]
    q_lat = dq * lax.rsqrt(jnp.mean(dq * dq, axis=-1, keepdims=True) + EPS) * qg_ref[...]
    kv_lat = dkv * lax.rsqrt(jnp.mean(dkv * dkv, axis=-1, keepdims=True) + EPS) * kvg_ref[...]
    qm = _mm(q_lat, wuq_ref[...])
    kn = _mm(kv_lat, wuk_ref[...])
    v_row = lax.broadcasted_iota(jnp.int32, (MLA_HEADS * LANES, 1), 0)
    v_ones = ((v_row % LANES) >= MLA_V).astype(F32)
    v_ref[0] = (_mm_nt(wuv_ref[...], kv_lat) + v_ones).astype(BF16)

    lane = lax.broadcasted_iota(jnp.int32, slab.shape, 1)
    half = MLA_ROPE // 2
    ang = pos_ref[0] * invf_ref[...]
    cosf = jnp.cos(ang)
    sinf = jnp.sin(ang)
    sin_signed = jnp.where(lane < MLA_NOPE + half, -sinf, sinf)
    k_rope = jnp.where((lane >= MLA_NOPE) & (lane < MLA_QK), slab, 0.0)

    def head_norm_rope(t, gain):
        t = t * lax.rsqrt(jnp.sum(t * t, axis=-1, keepdims=True) * (1.0 / MLA_QK) + EPS) * gain
        partner = jnp.where(lane < MLA_NOPE + half,
                            pltpu.roll(t, LANES - half, 1), pltpu.roll(t, half, 1))
        return t * cosf + partner * sin_signed

    scale = MLA_QK ** -0.5 * math.log2(math.e)
    for h in range(MLA_HEADS):
        sl = slice(h * LANES, (h + 1) * LANES)
        q_ref[0, :, sl] = (head_norm_rope(qm[:, sl], qhg_ref[...]) * scale).astype(BF16)
        k_ref[0, :, sl] = head_norm_rope(kn[:, sl] + k_rope, khg_ref[...]).astype(BF16)


def _mla_prep_call(lat, positions, q_norm_gain, kv_norm_gain, w_uq, w_ukv, q_head_gain, k_head_gain):
    B, S, n_lat = lat.shape
    H = MLA_HEADS
    tm = min(512, S)
    wuq = jnp.pad(w_uq.reshape(MLA_Q_RANK, H, MLA_QK), ((0, 0), (0, 0), (0, LANES - MLA_QK)))
    wuq = wuq.reshape(MLA_Q_RANK, H * LANES).astype(BF16)
    wkv = w_ukv.reshape(MLA_KV_RANK, H, MLA_NOPE + MLA_V)
    wuk = jnp.pad(wkv[:, :, :MLA_NOPE], ((0, 0), (0, 0), (0, LANES - MLA_NOPE)))
    wuk = wuk.reshape(MLA_KV_RANK, H * LANES).astype(BF16)
    wuv = jnp.pad(wkv[:, :, MLA_NOPE:], ((0, 0), (0, 0), (0, LANES - MLA_V)))
    wuv = wuv.reshape(MLA_KV_RANK, H * LANES).T.astype(BF16)
    pad_gain = lambda g: jnp.pad(g.reshape(1, MLA_QK), ((0, 0), (0, LANES - MLA_QK)))
    inv_freq = ROPE_THETA ** (-jnp.arange(0, MLA_ROPE, 2, dtype=F32) / MLA_ROPE)
    invf = jnp.concatenate([jnp.zeros((MLA_NOPE,), F32), inv_freq, inv_freq,
                            jnp.zeros((LANES - MLA_QK,), F32)]).reshape(1, LANES)
    pos = positions.astype(F32).reshape(B, S, 1)
    const = lambda shape: pl.BlockSpec(shape, lambda b, i: (0,) * len(shape))
    return pl.pallas_call(
        _mla_prep_kernel,
        grid=(B, S // tm),
        in_specs=[pl.BlockSpec((1, tm, n_lat), lambda b, i: (b, i, 0)),
                  pl.BlockSpec((1, tm, 1), lambda b, i: (b, i, 0)),
                  const((1, LANES)), const((1, MLA_Q_RANK)), const((1, MLA_KV_RANK)),
                  const((MLA_Q_RANK, H * LANES)), const((MLA_KV_RANK, H * LANES)),
                  const((H * LANES, MLA_KV_RANK)), const((1, LANES)), const((1, LANES))],
        out_specs=[pl.BlockSpec((1, tm, H * LANES), lambda b, i: (b, i, 0)),
                   pl.BlockSpec((1, tm, H * LANES), lambda b, i: (b, i, 0)),
                   pl.BlockSpec((1, H * LANES, tm), lambda b, i: (b, 0, i))],
        out_shape=[jax.ShapeDtypeStruct((B, S, H * LANES), BF16)] * 2
        + [jax.ShapeDtypeStruct((B, H * LANES, S), BF16)],
        compiler_params=_params("parallel", "parallel"),
        name="mla_prep",
    )(lat, pos, invf, q_norm_gain.reshape(1, -1), kv_norm_gain.reshape(1, -1), wuq, wuk, wuv,
      pad_gain(q_head_gain), pad_gain(k_head_gain))


def _flash_kernel(q_ref, k_ref, vt_ref, o_ref, *, tq, tk):
    qi = pl.program_id(2)
    q = q_ref[0]
    q0, q1 = q[:, :LANES], q[:, LANES:]
    key = lax.broadcasted_iota(jnp.int32, (tk, tq), 0)
    qry = lax.broadcasted_iota(jnp.int32, (tk, tq), 1)

    def scores(kj):
        start = pl.multiple_of(kj * tk, tk)
        rows = pl.ds(start, tk)
        return (_mm_nt(k_ref[0, rows, :LANES], q0),
                _mm_nt(k_ref[0, rows, LANES:], q1))

    def online(st, m, acc, vt):
        m_new = jnp.maximum(m, jnp.max(st, axis=0, keepdims=True))
        pt = jnp.exp2(st - m_new)
        return m_new, jnp.exp2(m - m_new) * acc + _mm(vt, pt)

    def softmax_pv(kj, s0, s1, m0, m1, acc0, acc1):
        start = pl.multiple_of(kj * tk, tk)
        cols = pl.ds(start, tk)
        m0, acc0 = online(s0, m0, acc0, vt_ref[0, :LANES, cols])
        m1, acc1 = online(s1, m1, acc1, vt_ref[0, LANES:, cols])
        return m0, m1, acc0, acc1

    def group(t, carry):
        blocks = [FLASH_UNROLL * t + u for u in range(FLASH_UNROLL)]
        ss = [scores(kj) for kj in blocks]
        for kj, s in zip(blocks, ss):
            carry = softmax_pv(kj, *s, *carry)
        return carry

    def single(kj, carry):
        return softmax_pv(kj, *scores(kj), *carry)

    neg = jnp.full((1, tq), NEG_INF, F32)
    zero = jnp.zeros((LANES, tq), F32)
    n_full = qi * (tq // tk)
    n_group = n_full // FLASH_UNROLL
    carry = lax.fori_loop(0, n_group, group, (neg, neg, zero, zero))
    carry = lax.fori_loop(n_group * FLASH_UNROLL, n_full, single, carry)
    for d in range(tq // tk):
        s0, s1 = scores(n_full + d)
        mask = key + d * tk <= qry
        carry = softmax_pv(n_full + d, jnp.where(mask, s0, NEG_INF), jnp.where(mask, s1, NEG_INF), *carry)
    _, _, acc0, acc1 = carry
    out_t = jnp.concatenate([acc0[:MLA_V] / acc0[MLA_V:MLA_V + 1],
                             acc1[:MLA_V] / acc1[MLA_V:MLA_V + 1]], axis=0)
    o_ref[0] = out_t.T.astype(o_ref.dtype)


def _flash_call(q, k, v):
    B, S, _ = q.shape
    tq = min(FLASH_TQ, S)
    pairs = MLA_HEADS // 2
    return pl.pallas_call(
        functools.partial(_flash_kernel, tq=tq, tk=min(FLASH_TK, tq)),
        grid=(B, pairs, S // tq),
        in_specs=[pl.BlockSpec((1, tq, 2 * LANES), lambda b, p, i: (b, i, p)),
                  pl.BlockSpec((1, S, 2 * LANES), lambda b, p, i: (b, 0, p)),
                  pl.BlockSpec((1, 2 * LANES, S), lambda b, p, i: (b, p, 0))],
        out_specs=pl.BlockSpec((1, tq, 2 * MLA_V), lambda b, p, i: (b, i, p)),
        out_shape=jax.ShapeDtypeStruct((B, S, MLA_HEADS * MLA_V), BF16),
        compiler_params=_params("parallel", "parallel", "arbitrary"),
        name="flash",
    )(q, k, v)


def _mix_mlp_kernel(*refs, n_in):
    ins = refs[:n_in]
    (x_ref, g1_ref, sh_ref, sc_ref, g2_ref, wo_ref, wu_ref, wd_ref, o_ref,
     x1_ref, h_ref, acc_ref) = refs[n_in:]
    f = pl.program_id(2)

    @pl.when(f == 0)
    def _first():
        y = None
        off = 0
        for r in ins:
            n = r.shape[-1]
            part = _mm(r[0], wo_ref[off:off + n, :])
            y = part if y is None else y + part
            off += n
        x1 = x_ref[0] + g1_ref[0] * y
        x1_ref[...] = x1
        h_ref[...] = _adaln(x1, sh_ref[0], sc_ref[0]).astype(BF16)
        acc_ref[...] = jnp.zeros_like(acc_ref)

    u = jnp.maximum(jnp.dot(h_ref[...], wu_ref[...], preferred_element_type=F32), 0.0)
    acc_ref[...] += _mm(u * u, wd_ref[...])

    @pl.when(f == pl.num_programs(2) - 1)
    def _last():
        o_ref[0] = x1_ref[...] + g2_ref[0] * acc_ref[...]


def _mix_mlp_call(ins, x, gate1, shift, scale, gate2, w_out, w_up, w_down):
    B, S, D = x.shape
    F = w_up.shape[1]
    tm = min(1024, S)
    tf = 1024
    n_in = len(ins)
    vec = pl.BlockSpec((1, 1, D), lambda b, i, f: (b, 0, 0))
    rows = lambda n: pl.BlockSpec((1, tm, n), lambda b, i, f: (b, i, 0))
    return pl.pallas_call(
        functools.partial(_mix_mlp_kernel, n_in=n_in),
        grid=(B, S // tm, F // tf),
        in_specs=[rows(a.shape[-1]) for a in ins] + [rows(D), vec, vec, vec, vec,
                  pl.BlockSpec(w_out.shape, lambda b, i, f: (0, 0)),
                  pl.BlockSpec((D, tf), lambda b, i, f: (0, f)),
                  pl.BlockSpec((tf, D), lambda b, i, f: (f, 0))],
        out_specs=rows(D),
        out_shape=jax.ShapeDtypeStruct((B, S, D), F32),
        scratch_shapes=[pltpu.VMEM((tm, D), F32), pltpu.VMEM((tm, D), BF16), pltpu.VMEM((tm, D), F32)],
        compiler_params=_params("parallel", "parallel", "arbitrary"),
        name="mix_mlp",
    )(*ins, x, gate1, shift, scale, gate2, w_out, w_up, w_down)


def _rwkv_proj_kernel(x_ref, xp_ref, sh_ref, sc_ref, mu_ref, wr_ref, wk_ref, wv_ref,
                      w1_ref, w2_ref, a1_ref, a2_ref, g1_ref, g2_ref, w0_ref, a0_ref,
                      r_ref, w_ref, k_ref, v_ref, a_ref, g_ref):
    h = _adaln(x_ref[0], sh_ref[0], sc_ref[0])
    tm = h.shape[0]
    prev_tail = _adaln(xp_ref[0], sh_ref[0], sc_ref[0])[SUBLANES - 1:SUBLANES, :]
    prev_tail = jnp.where(pl.program_id(1) == 0, 0.0, prev_tail)
    row = lax.broadcasted_iota(jnp.int32, h.shape, 0)
    h_prev = jnp.where(row == 0, prev_tail, pltpu.roll(h, 1, 0))
    xx = h_prev - h
    mix = lambda j: h + xx * mu_ref[j:j + 1, :]
    xr, xw, xk, xv, xa, xg = (mix(j) for j in range(6))
    r_ref[0] = _mm(xr, wr_ref[...]).astype(r_ref.dtype)
    k_ref[0] = _mm(xk, wk_ref[...]).astype(k_ref.dtype)
    v_ref[0] = _mm(xv, wv_ref[...]).astype(v_ref.dtype)
    wl = w0_ref[...] + _mm(jnp.tanh(_mm(xw, w1_ref[...])), w2_ref[...])
    w_log = -_softplus(-wl) - 0.5
    w_ref[0] = -jnp.exp(w_log)
    a_ref[0] = jax.nn.sigmoid(a0_ref[...] + _mm(_mm(xa, a1_ref[...]), a2_ref[...])).astype(a_ref.dtype)
    g_ref[0] = _mm(jax.nn.sigmoid(_mm(xg, g1_ref[...])), g2_ref[...]).astype(g_ref.dtype)


def _pad_lora(w_a, w_b):
    r = w_a.shape[1]
    rp = -(-r // LANES) * LANES
    return (jnp.pad(w_a, ((0, 0), (0, rp - r))).astype(BF16),
            jnp.pad(w_b, ((0, rp - r), (0, 0))).astype(BF16))


def _rwkv_proj_call(x, shift, scale, mu, w_r, w_k, w_v, w1, w2, a1, a2, g1, g2, w0, a0):
    B, S, D = x.shape
    tm = min(512, S)
    w1p, w2p = _pad_lora(w1, w2)
    a1p, a2p = _pad_lora(a1, a2)
    g1p, g2p = _pad_lora(g1, g2)
    const = lambda t: pl.BlockSpec(t.shape, lambda b, i: (0,) * t.ndim)
    vec = pl.BlockSpec((1, 1, D), lambda b, i: (b, 0, 0))
    row_blocks = tm // SUBLANES
    weights = (mu, w_r.astype(BF16), w_k.astype(BF16), w_v.astype(BF16), w1p, w2p, a1p, a2p,
               g1p, g2p, w0.reshape(1, D), a0.reshape(1, D))
    out = pl.BlockSpec((1, tm, D), lambda b, i: (b, i, 0))
    return pl.pallas_call(
        _rwkv_proj_kernel,
        grid=(B, S // tm),
        in_specs=[pl.BlockSpec((1, tm, D), lambda b, i: (b, i, 0)),
                  pl.BlockSpec((1, SUBLANES, D),
                               lambda b, i: (b, jnp.maximum(i * row_blocks - 1, 0), 0)),
                  vec, vec] + [const(t) for t in weights],
        out_specs=[out] * 6,
        out_shape=[jax.ShapeDtypeStruct((B, S, D), F32 if i == 1 else BF16) for i in range(6)],
        compiler_params=_params("parallel", "parallel"),
        name="rwkv_proj",
    )(x, x, shift, scale, *weights)


def _rwkv_kernel(r_ref, w_ref, k_ref, v_ref, a_ref, g_ref, kk_ref, ka_ref, rk_ref, lng_ref, lnb_ref,
                 o_ref, st_ref):
    C, Dh = CHUNK, RWKV_HEAD_DIM
    W2 = 2 * Dh
    pairs = range(RWKV_HEADS // 2)

    @pl.when(pl.program_id(1) == 0)
    def _init():
        st_ref[...] = jnp.zeros_like(st_ref)

    R = r_ref.shape[1]
    incl, strict, eye_p, first = _pair_masks(C)
    tri = _tri_masks(C)[0].astype(BF16)
    r2 = lax.broadcasted_iota(jnp.int32, (W2, W2), 0)
    c2 = lax.broadcasted_iota(jnp.int32, (W2, W2), 1)
    same_head = (r2 // Dh) == (c2 // Dh)
    ones_bd = same_head.astype(BF16)
    eye2 = (r2 == c2).astype(BF16)
    inv_dh = 1.0 / Dh

    units = []
    for c in range(R // C):
        rows = slice(c * C, (c + 1) * C)
        w = w_ref[0, rows, :]
        G = _tri_cumsum(tri, w)
        g_last = G[C - 1:C, :]
        e_in = jnp.exp(G)
        e_ex = jnp.exp(G - w)
        e_neg = jnp.exp(-G)
        e_rest = jnp.exp(g_last - G)
        e_last_col = jnp.broadcast_to(jnp.exp(g_last), (SUBLANES, g_last.shape[1])).T
        r = r_ref[0, rows, :].astype(F32)
        k0 = k_ref[0, rows, :].astype(F32)
        a = a_ref[0, rows, :].astype(F32)
        kk_raw = k0 * kk_ref[...]
        k_mod = k0 * (1.0 + (a - 1.0) * ka_ref[...])
        kk_slabs = [kk_raw[:, p * W2:(p + 1) * W2] for p in pairs]
        kk_sums = _head_sums([t * t for t in kk_slabs], ones_bd, split=True)
        for p in pairs:
            sl = slice(p * W2, (p + 1) * W2)
            kk = kk_slabs[p] * lax.rsqrt(kk_sums[p] + 1e-12)
            kh, rh, vh = k_mod[:, sl], r[:, sl], v_ref[0, rows, sl].astype(F32)
            bv = kk * a[:, sl]
            units.append(dict(
                c=c, sl=sl, rh=rh, kh=kh, vh=vh,
                ar=jnp.concatenate([-kk * e_ex[:, sl], rh * e_in[:, sl]], axis=0),
                b_t=bv * e_neg[:, sl], k_t=kh * e_neg[:, sl],
                bk_rest=jnp.concatenate([bv * e_rest[:, sl], kh * e_rest[:, sl]], axis=0),
                e_last=e_last_col[sl, 0:1]))
    def independent(us):
        if not us:
            return
        for u in us:
            ab = _mm_nt(u["ar"], _bd_rows(u["b_t"], first))
            ak = _mm_nt(u["ar"], _bd_rows(u["k_t"], first))
            u["L"] = -jnp.where(strict, ab[:C], 0.0)
            u["a_rb"] = jnp.where(incl, ab[C:], 0.0)
            u["akrk"] = jnp.concatenate([jnp.where(strict, ak[:C], 0.0), jnp.where(incl, ak[C:], 0.0)],
                                        axis=0)
            u["bk_rest_t"] = _mm_nt(eye2, u["bk_rest"])
        yield
        for u in us:
            u["kv"] = _mm(u["akrk"], _bd_rows(u["vh"], first))
        yield
        yield from _unit_lower_inv_pairs(us, eye_p, first)

    n_pairs = len(pairs)
    S = [st_ref[p] for p in pairs]

    def chain(chunks):
        for c in chunks:
            cu = units[c * n_pairs:(c + 1) * n_pairs]
            for p, u in zip(pairs, cu):
                u["sd"] = _mm(u["ar"], S[p]) + u["kv"]
            yield
            for u in cu:
                u["P"] = _mm(u["T"], _bd_rows(u["sd"][:C], first))
            yield
            for p, u in zip(pairs, cu):
                pv = jnp.concatenate([u["P"], u["vh"]], axis=0)
                S[p] = S[p] * u["e_last"] + jnp.where(same_head, _mm(u["bk_rest_t"], pv), 0.0)
            yield

    n_chunks = R // C
    half = max(1, n_chunks // 2)
    _emit(independent(units[:half * n_pairs]))
    _emit(independent(units[half * n_pairs:]), chain(range(half)))
    _emit(chain(range(half, n_chunks)))
    for p in pairs:
        st_ref[p] = S[p]

    for u in units:
        u["y"] = u["sd"][C:] + _mm(u["a_rb"], _bd_rows(u["P"], first))
    sums = _head_sums([u["y"] for u in units] + [u["rh"] * u["kh"] * rk_ref[:, u["sl"]] for u in units],
                      ones_bd)
    for u, s_y, s_rk in zip(units, sums[:len(units)], sums[len(units):]):
        u["yc"] = u["y"] - s_y * inv_dh
        u["bonus"] = s_rk * u["vh"]
    for u, s_var in zip(units, _head_sums([u["yc"] * u["yc"] for u in units], ones_bd)):
        sl = u["sl"]
        rows = slice(u["c"] * C, (u["c"] + 1) * C)
        yn = u["yc"] * lax.rsqrt(s_var * inv_dh + RWKV_GN_EPS) * lng_ref[:, sl] + lnb_ref[:, sl]
        o_ref[0, rows, sl] = ((yn + u["bonus"]) * g_ref[0, rows, sl].astype(F32)).astype(o_ref.dtype)


def _rwkv_call(r, w, k, v, a, g, k_k, k_a, r_k, ln_gain, ln_bias):
    B, S, D = r.shape
    H, Dh = RWKV_HEADS, RWKV_HEAD_DIM
    C = CHUNK * CHUNKS_PER_STEP
    seq = pl.BlockSpec((1, C, D), lambda b, c: (b, c, 0))
    vec = pl.BlockSpec((1, D), lambda b, c: (0, 0))
    flat = lambda t: t.reshape(1, D)
    return pl.pallas_call(
        _rwkv_kernel,
        grid=(B, S // C),
        in_specs=[seq] * 6 + [vec] * 5,
        out_specs=seq,
        out_shape=jax.ShapeDtypeStruct((B, S, D), BF16),
        scratch_shapes=[pltpu.VMEM((H // 2, 2 * Dh, 2 * Dh), F32)],
        compiler_params=_params("parallel", "arbitrary"),
        name="rwkv",
    )(r, w, k, v, a, g, flat(k_k), flat(k_a), flat(r_k), flat(ln_gain), flat(ln_bias))


def kernel(x, c, positions, w_mod, b_mod, w_in0, gdn_conv_w, gdn_a_log, gdn_dt_bias, gdn_norm_gain, mla_q_norm_gain, mla_kv_norm_gain, mla_w_uq, mla_w_ukv, mla_q_head_gain, mla_k_head_gain, w_out0, rwkv_mu, rwkv_w_r, rwkv_w_k, rwkv_w_v, rwkv_w_o, rwkv_w0, rwkv_w1, rwkv_w2, rwkv_a0, rwkv_a1, rwkv_a2, rwkv_g1, rwkv_g2, rwkv_k_k, rwkv_k_a, rwkv_r_k, rwkv_ln_gain, rwkv_ln_bias, w_up, w_down):
    B, S, D = x.shape
    mod = _mod_call(c, w_mod, b_mod)

    def mod_chunks(layer):
        return [mod[layer, :, i * D:(i + 1) * D].reshape(B, 1, D) for i in range(6)]

    shift1, scale1, gate1, shift2, scale2, gate2 = mod_chunks(0)
    qkvz, lat = _inproj_call(x, shift1, scale1, _pack_w_in(w_in0[0]))
    o_gdn = _gdn_call(qkvz, lat, gdn_conv_w[0], gdn_a_log[0], gdn_dt_bias[0], gdn_norm_gain[0])
    q, k, v = _mla_prep_call(lat, positions, mla_q_norm_gain[0], mla_kv_norm_gain[0], mla_w_uq[0],
                             mla_w_ukv[0], mla_q_head_gain[0], mla_k_head_gain[0])
    o_mla = _flash_call(q, k, v)
    x = _mix_mlp_call([o_gdn, o_mla], x, gate1, shift2, scale2, gate2, w_out0[0].astype(BF16),
                      w_up[0].astype(BF16), w_down[0].astype(BF16))

    shift1, scale1, gate1, shift2, scale2, gate2 = mod_chunks(1)
    r, w, k, v, a, g = _rwkv_proj_call(x, shift1, scale1, rwkv_mu[0], rwkv_w_r[0], rwkv_w_k[0],
                                       rwkv_w_v[0], rwkv_w1[0], rwkv_w2[0], rwkv_a1[0], rwkv_a2[0],
                                       rwkv_g1[0], rwkv_g2[0], rwkv_w0[0], rwkv_a0[0])
    y = _rwkv_call(r, w, k, v, a, g, rwkv_k_k[0], rwkv_k_a[0], rwkv_r_k[0], rwkv_ln_gain[0],
                   rwkv_ln_bias[0])
    x = _mix_mlp_call([y], x, gate1, shift2, scale2, gate2, rwkv_w_o[0].astype(BF16),
                      w_up[1].astype(BF16), w_down[1].astype(BF16))
    return x
```

```python
import functools
import math

import jax
import jax.numpy as jnp
from jax import lax
from jax.experimental import pallas as pl
from jax.experimental.pallas import tpu as pltpu

F32 = jnp.float32
BF16 = jnp.bfloat16

D_MODEL = 1024
DEPTH = 2
GDN_HEADS = 8
GDN_HEAD_DIM = 64
GDN_WIDTH = GDN_HEADS * GDN_HEAD_DIM
GDN_CONV = 4
CHUNK = 64
CHUNKS_PER_STEP = 4
FLASH_TQ = 1024
FLASH_TK = 256
FLASH_UNROLL = 4
MLA_HEADS = 8
MLA_Q_RANK = 256
MLA_KV_RANK = 128
MLA_NOPE = 64
MLA_ROPE = 32
MLA_QK = MLA_NOPE + MLA_ROPE
MLA_V = 64
ROPE_THETA = 10000.0
NEG_INF = -1e30
RWKV_HEADS = 16
RWKV_HEAD_DIM = 64
RWKV_GN_EPS = 64e-5
D_FF = 4 * D_MODEL
EPS = 1e-6
LANES = 128
SUBLANES = 8
VMEM_LIMIT = 56 * 1024 * 1024


def _params(*sem):
    return pltpu.CompilerParams(dimension_semantics=sem, vmem_limit_bytes=VMEM_LIMIT)


def _mm(a, b):
    return jnp.dot(a.astype(BF16), b.astype(BF16), preferred_element_type=F32)


def _mm_nt(a, b):
    return lax.dot_general(a.astype(BF16), b.astype(BF16), (((1,), (1,)), ((), ())),
                           preferred_element_type=F32)


def _tri_cumsum(tri_bf16, x):
    hi = x.astype(BF16)
    r1 = x - hi.astype(F32)
    mid = r1.astype(BF16)
    lo = (r1 - mid.astype(F32)).astype(BF16)
    dot = functools.partial(jnp.dot, preferred_element_type=F32)
    return dot(tri_bf16, hi) + dot(tri_bf16, mid) + dot(tri_bf16, lo)


def _unit_lower_inv(Ls, eye):
    n = Ls[0].shape[0]
    row = lax.broadcasted_iota(jnp.int32, (n, n), 0)
    col = lax.broadcasted_iota(jnp.int32, (n, n), 1)
    Ts = [eye] * len(Ls)
    s = 1
    while s < n:
        lower_left = ((row // (2 * s)) == (col // (2 * s))) & ((row // s) % 2 == 1) & ((col // s) % 2 == 0)
        Cs = [jnp.where(lower_left, L, 0.0) for L in Ls]
        if s == 1:
            Ts = [T - C for T, C in zip(Ts, Cs)]
        else:
            Xs = [_mm(T, C) for T, C in zip(Ts, Cs)]
            Ts = [T - _mm(X, T) for T, X in zip(Ts, Xs)]
        s *= 2
    return Ts


def _pair_masks(n):
    row = lax.broadcasted_iota(jnp.int32, (n, 2 * n), 0)
    lane = lax.broadcasted_iota(jnp.int32, (n, 2 * n), 1)
    col = lane % n
    return row >= col, row > col, (row == col).astype(F32), lane < n


def _bd_rows(x, first):
    x = x.astype(BF16)
    zero = jnp.zeros_like(x)
    return jnp.concatenate([jnp.where(first, x, zero), jnp.where(first, zero, x)], axis=0)


def _unit_lower_inv_pairs(units, eye_p, first):
    n = units[0]["L"].shape[0]
    row = lax.broadcasted_iota(jnp.int32, (n, 2 * n), 0)
    col = lax.broadcasted_iota(jnp.int32, (n, 2 * n), 1) % n
    Ts = [eye_p] * len(units)
    s = 1
    while s < n:
        lower_left = ((row // (2 * s)) == (col // (2 * s))) & ((row // s) % 2 == 1) & ((col // s) % 2 == 0)
        Cs = [jnp.where(lower_left, u["L"], 0.0) for u in units]
        if s == 1:
            Ts = [T - C for T, C in zip(Ts, Cs)]
        else:
            Xs = [_mm(T, _bd_rows(C, first)) for T, C in zip(Ts, Cs)]
            yield
            Ts = [T - _mm(X, _bd_rows(T, first)) for T, X in zip(Ts, Xs)]
            yield
        s *= 2
    for u, T in zip(units, Ts):
        u["T"] = T


def _emit(*gens):
    live = list(gens)
    while live:
        for g in list(live):
            try:
                next(g)
            except StopIteration:
                live.remove(g)


def _head_sums(xs, ones_bd, split=False):
    n = xs[0].shape[0]
    x = jnp.concatenate(xs, axis=0)
    hi = x.astype(BF16)
    out = jnp.dot(hi, ones_bd, preferred_element_type=F32)
    if split:
        out = out + jnp.dot((x - hi.astype(F32)).astype(BF16), ones_bd, preferred_element_type=F32)
    return [out[i * n:(i + 1) * n] for i in range(len(xs))]


def _softplus(x):
    return jnp.maximum(x, 0.0) + jnp.log1p(jnp.exp(-jnp.abs(x)))


def _silu(x):
    return x * jax.nn.sigmoid(x)


def _adaln(x, shift, scale):
    ms = jnp.mean(x * x, axis=-1, keepdims=True)
    return x * lax.rsqrt(ms + EPS) * (1.0 + scale) + shift


def _tri_masks(n):
    row = lax.broadcasted_iota(jnp.int32, (n, n), 0)
    col = lax.broadcasted_iota(jnp.int32, (n, n), 1)
    return row >= col, row > col, (row == col).astype(F32)


def _mod_kernel(c_ref, w_ref, b_ref, o_ref):
    cond = _silu(c_ref[...])
    o_ref[0] = jnp.dot(cond, w_ref[0], precision=lax.Precision.HIGHEST,
                       preferred_element_type=F32) + b_ref[0]


def _mod_call(c, w_mod, b_mod):
    B, D = c.shape
    depth, _, n = w_mod.shape
    tn = 1536
    c_pad = jnp.pad(c, ((0, SUBLANES - B), (0, 0)))
    out = pl.pallas_call(
        _mod_kernel,
        grid=(depth, n // tn),
        in_specs=[pl.BlockSpec((SUBLANES, D), lambda l, j: (0, 0)),
                  pl.BlockSpec((1, D, tn), lambda l, j: (l, 0, j)),
                  pl.BlockSpec((1, 1, tn), lambda l, j: (l, 0, j))],
        out_specs=pl.BlockSpec((1, SUBLANES, tn), lambda l, j: (l, 0, j)),
        out_shape=jax.ShapeDtypeStruct((depth, SUBLANES, n), F32),
        compiler_params=_params("parallel", "parallel"),
        name="mod",
    )(c_pad, w_mod, b_mod.reshape(depth, 1, n))
    return out[:, :B, :]


IN_PACKED = 2560


def _pack_w_in(w_in):
    D = w_in.shape[0]
    o_a = 4 * GDN_WIDTH
    o_dq = o_a + 2 * GDN_HEADS
    o_dkv = o_dq + MLA_Q_RANK
    o_kr = o_dkv + MLA_KV_RANK
    z = lambda n: jnp.zeros((D, n), w_in.dtype)
    return jnp.concatenate(
        [w_in[:, :o_a], w_in[:, o_dq:o_dkv], w_in[:, o_dkv:o_kr],
         w_in[:, o_a:o_dq], z(MLA_NOPE - 2 * GDN_HEADS), w_in[:, o_kr:o_kr + MLA_ROPE],
         z(LANES - MLA_NOPE - MLA_ROPE)], axis=1).astype(BF16)


def _inproj_kernel(x_ref, sh_ref, sc_ref, w_ref, qkvz_ref, lat_ref):
    h = _adaln(x_ref[0], sh_ref[0], sc_ref[0]).astype(BF16)
    y = jnp.dot(h, w_ref[...], preferred_element_type=F32)
    qkvz_ref[0] = y[:, :4 * GDN_WIDTH].astype(qkvz_ref.dtype)
    lat_ref[0] = y[:, 4 * GDN_WIDTH:]


def _inproj_call(x, shift, scale, w_packed):
    B, S, D = x.shape
    tm = min(1024, S)
    n_lat = IN_PACKED - 4 * GDN_WIDTH
    return pl.pallas_call(
        _inproj_kernel,
        grid=(B, S // tm),
        in_specs=[pl.BlockSpec((1, tm, D), lambda b, i: (b, i, 0)),
                  pl.BlockSpec((1, 1, D), lambda b, i: (b, 0, 0)),
                  pl.BlockSpec((1, 1, D), lambda b, i: (b, 0, 0)),
                  pl.BlockSpec((D, IN_PACKED), lambda b, i: (0, 0))],
        out_specs=[pl.BlockSpec((1, tm, 4 * GDN_WIDTH), lambda b, i: (b, i, 0)),
                   pl.BlockSpec((1, tm, n_lat), lambda b, i: (b, i, 0))],
        out_shape=[jax.ShapeDtypeStruct((B, S, 4 * GDN_WIDTH), BF16),
                   jax.ShapeDtypeStruct((B, S, n_lat), F32)],
        compiler_params=_params("parallel", "parallel"),
        name="inproj",
    )(x, shift, scale, w_packed)


def _gdn_kernel(x_ref, ab_ref, cw_ref, alog_ref, dtb_ref, gain_ref, o_ref, ext_ref, st_ref):
    C, H, Dh, W = CHUNK, GDN_HEADS, GDN_HEAD_DIM, GDN_WIDTH

    @pl.when(pl.program_id(1) == 0)
    def _init():
        ext_ref[0:SUBLANES, :] = jnp.zeros((SUBLANES, 3 * W), F32)
        st_ref[...] = jnp.zeros_like(st_ref)

    R = x_ref.shape[1]
    x = x_ref[0].astype(F32)
    ext_ref[SUBLANES:SUBLANES + R, :] = x[:, :3 * W]
    base = SUBLANES - (GDN_CONV - 1)
    conv = cw_ref[0:1, :] * ext_ref[base:base + R, :]
    for j in range(1, GDN_CONV):
        conv = conv + cw_ref[j:j + 1, :] * ext_ref[base + j:base + j + R, :]
    ext_ref[0:SUBLANES, :] = x[R - SUBLANES:R, :3 * W]
    qkv = _silu(conv)

    W2 = 2 * Dh
    pairs = range(H // 2)
    causal, strict, eye_p, first = _pair_masks(C)
    tri = _tri_masks(C)[0].astype(BF16)
    r2 = lax.broadcasted_iota(jnp.int32, (W2, W2), 0)
    c2 = lax.broadcasted_iota(jnp.int32, (W2, W2), 1)
    same_head = (r2 // Dh) == (c2 // Dh)
    ones_bd = same_head.astype(BF16)
    eye2 = (r2 == c2).astype(BF16)
    first_row = first[0:1, :]
    ab = ab_ref[0]
    beta_all = jax.nn.sigmoid(ab)
    g_all = -jnp.exp(alog_ref[...]) * _softplus(ab + dtb_ref[...])
    qk_slabs = [qkv[:, off + p * W2:off + (p + 1) * W2] for off in (0, W) for p in pairs]
    qk_norm = [t * lax.rsqrt(ss + 1e-12)
               for t, ss in zip(qk_slabs, _head_sums([t * t for t in qk_slabs], ones_bd, split=True))]
    q_n, k_n = qk_norm[:len(pairs)], qk_norm[len(pairs):]

    units = []
    for c in range(R // C):
        rows = slice(c * C, (c + 1) * C)
        gcum = _tri_cumsum(tri, g_all[rows])
        gcum_t = gcum.T
        eg = jnp.exp(gcum)
        glast = gcum[C - 1:C, :]
        eg_rest = jnp.exp(glast - gcum)
        eg_last = jnp.exp(glast)
        for p in pairs:
            h0, h1 = 2 * p, 2 * p + 1
            per_head = lambda t, off=0: jnp.where(first, t[:, off + h0:off + h0 + 1], t[:, off + h1:off + h1 + 1])
            q = q_n[p][rows] * (Dh ** -0.5)
            k = k_n[p][rows]
            bc = per_head(beta_all[rows], H)
            eg_p = per_head(eg)
            kb = k * bc
            g_col = jnp.concatenate([gcum_t[h0:h0 + 1, :], gcum_t[h1:h1 + 1, :]], axis=1)
            decay = jnp.where(causal, jnp.exp(jnp.where(causal, per_head(gcum) - g_col, 0.0)), 0.0)
            units.append(dict(
                c=c, p=p, decay=decay, vb=qkv[rows, 2 * W + p * W2:2 * W + (p + 1) * W2] * bc, k=k,
                kbq=jnp.concatenate([kb, q], axis=0),
                kbq_g=jnp.concatenate([kb * eg_p, q * eg_p], axis=0),
                k_rest=k * per_head(eg_rest),
                e_last=jnp.where(first_row, eg_last[:, h0:h0 + 1], eg_last[:, h1:h1 + 1])))
    def independent(us):
        if not us:
            return
        for u in us:
            kk = _mm_nt(u["kbq"], _bd_rows(u["k"], first))
            u["L"] = jnp.where(strict, kk[:C] * u["decay"], 0.0)
            u["qk"] = jnp.where(causal, kk[C:] * u["decay"], 0.0)
            u["k_rest_t"] = _mm_nt(eye2, u["k_rest"])
        yield
        yield from _unit_lower_inv_pairs(us, eye_p, first)

    n_pairs = len(pairs)
    S = [st_ref[p] for p in pairs]

    def chain(chunks):
        for c in chunks:
            cu = units[c * n_pairs:(c + 1) * n_pairs]
            for u in cu:
                u["ks"] = _mm(u["kbq_g"], S[u["p"]])
            yield
            for u in cu:
                u["v_new"] = _mm(u["T"], _bd_rows(u["vb"] - u["ks"][:C], first))
            yield
            for u in cu:
                S[u["p"]] = (S[u["p"]] * u["e_last"]
                             + jnp.where(same_head, _mm(u["k_rest_t"], u["v_new"]), 0.0))
            yield

    n_chunks = R // C
    half = max(1, n_chunks // 2)
    _emit(independent(units[:half * n_pairs]))
    _emit(independent(units[half * n_pairs:]), chain(range(half)))
    _emit(chain(range(half, n_chunks)))
    for p in pairs:
        st_ref[p] = S[p]

    for u in units:
        u["o"] = u["ks"][C:] + _mm(u["qk"], _bd_rows(u["v_new"], first))
    for u, ss in zip(units, _head_sums([u["o"] * u["o"] for u in units], ones_bd)):
        rows = slice(u["c"] * C, (u["c"] + 1) * C)
        sl = slice(u["p"] * W2, (u["p"] + 1) * W2)
        o = u["o"] * lax.rsqrt(ss * (1.0 / Dh) + EPS)
        z = x[rows, 3 * W + u["p"] * W2:3 * W + (u["p"] + 1) * W2]
        o_ref[0, rows, sl] = (o * gain_ref[...] * _silu(z)).astype(o_ref.dtype)


def _gdn_call(qkvz, lat, conv_w, a_log, dt_bias, gain):
    B, S, _ = qkvz.shape
    H, Dh, W = GDN_HEADS, GDN_HEAD_DIM, GDN_WIDTH
    C = CHUNK * CHUNKS_PER_STEP
    lane_pad = lambda t: jnp.pad(t.reshape(1, H), ((0, 0), (0, LANES - H)))
    ab_block = (lat.shape[-1] // LANES) - 1
    return pl.pallas_call(
        _gdn_kernel,
        grid=(B, S // C),
        in_specs=[pl.BlockSpec((1, C, 4 * W), lambda b, c: (b, c, 0)),
                  pl.BlockSpec((1, C, LANES), lambda b, c: (b, c, ab_block)),
                  pl.BlockSpec((GDN_CONV, 3 * W), lambda b, c: (0, 0)),
                  pl.BlockSpec((1, LANES), lambda b, c: (0, 0)),
                  pl.BlockSpec((1, LANES), lambda b, c: (0, 0)),
                  pl.BlockSpec((1, 2 * Dh), lambda b, c: (0, 0))],
        out_specs=pl.BlockSpec((1, C, W), lambda b, c: (b, c, 0)),
        out_shape=jax.ShapeDtypeStruct((B, S, W), BF16),
        scratch_shapes=[pltpu.VMEM((C + SUBLANES, 3 * W), F32),
                        pltpu.VMEM((H // 2, 2 * Dh, 2 * Dh), F32)],
        compiler_params=_params("parallel", "arbitrary"),
        name="gdn",
    )(qkvz, lat, conv_w, lane_pad(a_log), lane_pad(dt_bias), jnp.tile(gain.reshape(1, Dh), (1, 2)))


def _mla_prep_kernel(lat_ref, pos_ref, invf_ref, qg_ref, kvg_ref, wuq_ref, wuk_ref, wuv_ref,
                     qhg_ref, khg_ref, q_ref, k_ref, v_ref):
    lat = lat_ref[0]
    dq = lat[:, :MLA_Q_RANK]
    dkv = lat[:, MLA_Q_RANK:MLA_Q_RANK + MLA_KV_RANK]
    slab = lat[:, MLA_Q_RANK + MLA_KV_RANK:]
    q_lat = dq * lax.rsqrt(jnp.mean(dq * dq, axis=-1, keepdims=True) + EPS) * qg_ref[...]
    kv_lat = dkv * lax.rsqrt(jnp.mean(dkv * dkv, axis=-1, keepdims=True) + EPS) * kvg_ref[...]
    qm = _mm(q_lat, wuq_ref[...])
    kn = _mm(kv_lat, wuk_ref[...])
    v_row = lax.broadcasted_iota(jnp.int32, (MLA_HEADS * LANES, 1), 0)
    v_ones = ((v_row % LANES) >= MLA_V).astype(F32)
    v_ref[0] = (_mm_nt(wuv_ref[...], kv_lat) + v_ones).astype(BF16)

    lane = lax.broadcasted_iota(jnp.int32, slab.shape, 1)
    half = MLA_ROPE // 2
    ang = pos_ref[0] * invf_ref[...]
    cosf = jnp.cos(ang)
    sinf = jnp.sin(ang)
    sin_signed = jnp.where(lane < MLA_NOPE + half, -sinf, sinf)
    k_rope = jnp.where((lane >= MLA_NOPE) & (lane < MLA_QK), slab, 0.0)

    def head_norm_rope(t, gain):
        t = t * lax.rsqrt(jnp.sum(t * t, axis=-1, keepdims=True) * (1.0 / MLA_QK) + EPS) * gain
        partner = jnp.where(lane < MLA_NOPE + half,
                            pltpu.roll(t, LANES - half, 1), pltpu.roll(t, half, 1))
        return t * cosf + partner * sin_signed

    scale = MLA_QK ** -0.5 * math.log2(math.e)
    for h in range(MLA_HEADS):
        sl = slice(h * LANES, (h + 1) * LANES)
        q_ref[0, :, sl] = (head_norm_rope(qm[:, sl], qhg_ref[...]) * scale).astype(BF16)
        k_ref[0, :, sl] = head_norm_rope(kn[:, sl] + k_rope, khg_ref[...]).astype(BF16)


def _mla_prep_call(lat, positions, q_norm_gain, kv_norm_gain, w_uq, w_ukv, q_head_gain, k_head_gain):
    B, S, n_lat = lat.shape
    H = MLA_HEADS
    tm = min(512, S)
    wuq = jnp.pad(w_uq.reshape(MLA_Q_RANK, H, MLA_QK), ((0, 0), (0, 0), (0, LANES - MLA_QK)))
    wuq = wuq.reshape(MLA_Q_RANK, H * LANES).astype(BF16)
    wkv = w_ukv.reshape(MLA_KV_RANK, H, MLA_NOPE + MLA_V)
    wuk = jnp.pad(wkv[:, :, :MLA_NOPE], ((0, 0), (0, 0), (0, LANES - MLA_NOPE)))
    wuk = wuk.reshape(MLA_KV_RANK, H * LANES).astype(BF16)
    wuv = jnp.pad(wkv[:, :, MLA_NOPE:], ((0, 0), (0, 0), (0, LANES - MLA_V)))
    wuv = wuv.reshape(MLA_KV_RANK, H * LANES).T.astype(BF16)
    pad_gain = lambda g: jnp.pad(g.reshape(1, MLA_QK), ((0, 0), (0, LANES - MLA_QK)))
    inv_freq = ROPE_THETA ** (-jnp.arange(0, MLA_ROPE, 2, dtype=F32) / MLA_ROPE)
    invf = jnp.concatenate([jnp.zeros((MLA_NOPE,), F32), inv_freq, inv_freq,
                            jnp.zeros((LANES - MLA_QK,), F32)]).reshape(1, LANES)
    pos = positions.astype(F32).reshape(B, S, 1)
    const = lambda shape: pl.BlockSpec(shape, lambda b, i: (0,) * len(shape))
    return pl.pallas_call(
        _mla_prep_kernel,
        grid=(B, S // tm),
        in_specs=[pl.BlockSpec((1, tm, n_lat), lambda b, i: (b, i, 0)),
                  pl.BlockSpec((1, tm, 1), lambda b, i: (b, i, 0)),
                  const((1, LANES)), const((1, MLA_Q_RANK)), const((1, MLA_KV_RANK)),
                  const((MLA_Q_RANK, H * LANES)), const((MLA_KV_RANK, H * LANES)),
                  const((H * LANES, MLA_KV_RANK)), const((1, LANES)), const((1, LANES))],
        out_specs=[pl.BlockSpec((1, tm, H * LANES), lambda b, i: (b, i, 0)),
                   pl.BlockSpec((1, tm, H * LANES), lambda b, i: (b, i, 0)),
                   pl.BlockSpec((1, H * LANES, tm), lambda b, i: (b, 0, i))],
        out_shape=[jax.ShapeDtypeStruct((B, S, H * LANES), BF16)] * 2
        + [jax.ShapeDtypeStruct((B, H * LANES, S), BF16)],
        compiler_params=_params("parallel", "parallel"),
        name="mla_prep",
    )(lat, pos, invf, q_norm_gain.reshape(1, -1), kv_norm_gain.reshape(1, -1), wuq, wuk, wuv,
      pad_gain(q_head_gain), pad_gain(k_head_gain))


def _flash_kernel(q_ref, k_ref, vt_ref, o_ref, *, tq, tk):
    qi = pl.program_id(2)
    q = q_ref[0]
    q0, q1 = q[:, :LANES], q[:, LANES:]
    key = lax.broadcasted_iota(jnp.int32, (tk, tq), 0)
    qry = lax.broadcasted_iota(jnp.int32, (tk, tq), 1)

    def scores(kj, c0=0):
        start = pl.multiple_of(kj * tk, tk)
        rows = pl.ds(start, tk)
        return (_mm_nt(k_ref[0, rows, :LANES], q0[c0:]),
                _mm_nt(k_ref[0, rows, LANES:], q1[c0:]))

    def online(st, m, acc, vt):
        m_new = jnp.maximum(m, jnp.max(st, axis=0, keepdims=True))
        pt = jnp.exp2(st - m_new)
        return m_new, jnp.exp2(m - m_new) * acc + _mm(vt, pt)

    def softmax_pv(kj, s0, s1, m0, m1, acc0, acc1):
        start = pl.multiple_of(kj * tk, tk)
        cols = pl.ds(start, tk)
        m0, acc0 = online(s0, m0, acc0, vt_ref[0, :LANES, cols])
        m1, acc1 = online(s1, m1, acc1, vt_ref[0, LANES:, cols])
        return m0, m1, acc0, acc1

    def group(t, carry):
        blocks = [FLASH_UNROLL * t + u for u in range(FLASH_UNROLL)]
        ss = [scores(kj) for kj in blocks]
        for kj, s in zip(blocks, ss):
            carry = softmax_pv(kj, *s, *carry)
        return carry

    def single(kj, carry):
        return softmax_pv(kj, *scores(kj), *carry)

    neg = jnp.full((1, tq), NEG_INF, F32)
    zero = jnp.zeros((LANES, tq), F32)
    n_full = qi * (tq // tk)
    n_group = n_full // FLASH_UNROLL
    carry = lax.fori_loop(0, n_group, group, (neg, neg, zero, zero))
    carry = lax.fori_loop(n_group * FLASH_UNROLL, n_full, single, carry)
    m0, m1, acc0, acc1 = carry
    for d in range(tq // tk):
        c0 = d * tk
        s0, s1 = scores(n_full + d, c0)
        mask = (key <= qry)[:, :tq - c0]
        sub = softmax_pv(n_full + d, jnp.where(mask, s0, NEG_INF), jnp.where(mask, s1, NEG_INF),
                         m0[:, c0:], m1[:, c0:], acc0[:, c0:], acc1[:, c0:])
        m0, m1, acc0, acc1 = (jnp.concatenate([old[:, :c0], new], axis=1) if c0 else new
                              for old, new in zip((m0, m1, acc0, acc1), sub))
    out_t = jnp.concatenate([acc0[:MLA_V] / acc0[MLA_V:MLA_V + 1],
                             acc1[:MLA_V] / acc1[MLA_V:MLA_V + 1]], axis=0)
    o_ref[0] = out_t.T.astype(o_ref.dtype)


def _flash_call(q, k, v):
    B, S, _ = q.shape
    tq = min(FLASH_TQ, S)
    pairs = MLA_HEADS // 2
    return pl.pallas_call(
        functools.partial(_flash_kernel, tq=tq, tk=min(FLASH_TK, tq)),
        grid=(B, pairs, S // tq),
        in_specs=[pl.BlockSpec((1, tq, 2 * LANES), lambda b, p, i: (b, i, p)),
                  pl.BlockSpec((1, S, 2 * LANES), lambda b, p, i: (b, 0, p)),
                  pl.BlockSpec((1, 2 * LANES, S), lambda b, p, i: (b, p, 0))],
        out_specs=pl.BlockSpec((1, tq, 2 * MLA_V), lambda b, p, i: (b, i, p)),
        out_shape=jax.ShapeDtypeStruct((B, S, MLA_HEADS * MLA_V), BF16),
        compiler_params=_params("parallel", "parallel", "arbitrary"),
        name="flash",
    )(q, k, v)


def _mix_mlp_kernel(*refs, n_in):
    ins = refs[:n_in]
    (x_ref, g1_ref, sh_ref, sc_ref, g2_ref, wo_ref, wu_ref, wd_ref, o_ref,
     x1_ref, h_ref, acc_ref) = refs[n_in:]
    f = pl.program_id(2)

    @pl.when(f == 0)
    def _first():
        y = None
        off = 0
        for r in ins:
            n = r.shape[-1]
            part = _mm(r[0], wo_ref[off:off + n, :])
            y = part if y is None else y + part
            off += n
        x1 = x_ref[0] + g1_ref[0] * y
        x1_ref[...] = x1
        h_ref[...] = _adaln(x1, sh_ref[0], sc_ref[0]).astype(BF16)
        acc_ref[...] = jnp.zeros_like(acc_ref)

    u = jnp.maximum(jnp.dot(h_ref[...], wu_ref[...], preferred_element_type=F32), 0.0)
    acc_ref[...] += _mm(u * u, wd_ref[...])

    @pl.when(f == pl.num_programs(2) - 1)
    def _last():
        o_ref[0] = x1_ref[...] + g2_ref[0] * acc_ref[...]


def _mix_mlp_call(ins, x, gate1, shift, scale, gate2, w_out, w_up, w_down):
    B, S, D = x.shape
    F = w_up.shape[1]
    tm = min(1024, S)
    tf = 1024
    n_in = len(ins)
    vec = pl.BlockSpec((1, 1, D), lambda b, i, f: (b, 0, 0))
    rows = lambda n: pl.BlockSpec((1, tm, n), lambda b, i, f: (b, i, 0))
    return pl.pallas_call(
        functools.partial(_mix_mlp_kernel, n_in=n_in),
        grid=(B, S // tm, F // tf),
        in_specs=[rows(a.shape[-1]) for a in ins] + [rows(D), vec, vec, vec, vec,
                  pl.BlockSpec(w_out.shape, lambda b, i, f: (0, 0)),
                  pl.BlockSpec((D, tf), lambda b, i, f: (0, f)),
                  pl.BlockSpec((tf, D), lambda b, i, f: (f, 0))],
        out_specs=rows(D),
        out_shape=jax.ShapeDtypeStruct((B, S, D), F32),
        scratch_shapes=[pltpu.VMEM((tm, D), F32), pltpu.VMEM((tm, D), BF16), pltpu.VMEM((tm, D), F32)],
        compiler_params=_params("parallel", "parallel", "arbitrary"),
        name="mix_mlp",
    )(*ins, x, gate1, shift, scale, gate2, w_out, w_up, w_down)


def _rwkv_proj_kernel(x_ref, xp_ref, sh_ref, sc_ref, mu_ref, wr_ref, wk_ref, wv_ref,
                      w1_ref, w2_ref, a1_ref, a2_ref, g1_ref, g2_ref, w0_ref, a0_ref,
                      r_ref, w_ref, k_ref, v_ref, a_ref, g_ref):
    h = _adaln(x_ref[0], sh_ref[0], sc_ref[0])
    tm = h.shape[0]
    prev_tail = _adaln(xp_ref[0], sh_ref[0], sc_ref[0])[SUBLANES - 1:SUBLANES, :]
    prev_tail = jnp.where(pl.program_id(1) == 0, 0.0, prev_tail)
    row = lax.broadcasted_iota(jnp.int32, h.shape, 0)
    h_prev = jnp.where(row == 0, prev_tail, pltpu.roll(h, 1, 0))
    xx = h_prev - h
    mix = lambda j: h + xx * mu_ref[j:j + 1, :]
    xr, xw, xk, xv, xa, xg = (mix(j) for j in range(6))
    r_ref[0] = _mm(xr, wr_ref[...]).astype(r_ref.dtype)
    k_ref[0] = _mm(xk, wk_ref[...]).astype(k_ref.dtype)
    v_ref[0] = _mm(xv, wv_ref[...]).astype(v_ref.dtype)
    wl = w0_ref[...] + _mm(jnp.tanh(_mm(xw, w1_ref[...])), w2_ref[...])
    w_log = -_softplus(-wl) - 0.5
    w_ref[0] = -jnp.exp(w_log)
    a_ref[0] = jax.nn.sigmoid(a0_ref[...] + _mm(_mm(xa, a1_ref[...]), a2_ref[...])).astype(a_ref.dtype)
    g_ref[0] = _mm(jax.nn.sigmoid(_mm(xg, g1_ref[...])), g2_ref[...]).astype(g_ref.dtype)


def _pad_lora(w_a, w_b):
    r = w_a.shape[1]
    rp = -(-r // LANES) * LANES
    return (jnp.pad(w_a, ((0, 0), (0, rp - r))).astype(BF16),
            jnp.pad(w_b, ((0, rp - r), (0, 0))).astype(BF16))


def _rwkv_proj_call(x, shift, scale, mu, w_r, w_k, w_v, w1, w2, a1, a2, g1, g2, w0, a0):
    B, S, D = x.shape
    tm = min(512, S)
    w1p, w2p = _pad_lora(w1, w2)
    a1p, a2p = _pad_lora(a1, a2)
    g1p, g2p = _pad_lora(g1, g2)
    const = lambda t: pl.BlockSpec(t.shape, lambda b, i: (0,) * t.ndim)
    vec = pl.BlockSpec((1, 1, D), lambda b, i: (b, 0, 0))
    row_blocks = tm // SUBLANES
    weights = (mu, w_r.astype(BF16), w_k.astype(BF16), w_v.astype(BF16), w1p, w2p, a1p, a2p,
               g1p, g2p, w0.reshape(1, D), a0.reshape(1, D))
    out = pl.BlockSpec((1, tm, D), lambda b, i: (b, i, 0))
    return pl.pallas_call(
        _rwkv_proj_kernel,
        grid=(B, S // tm),
        in_specs=[pl.BlockSpec((1, tm, D), lambda b, i: (b, i, 0)),
                  pl.BlockSpec((1, SUBLANES, D),
                               lambda b, i: (b, jnp.maximum(i * row_blocks - 1, 0), 0)),
                  vec, vec] + [const(t) for t in weights],
        out_specs=[out] * 6,
        out_shape=[jax.ShapeDtypeStruct((B, S, D), F32 if i == 1 else BF16) for i in range(6)],
        compiler_params=_params("parallel", "parallel"),
        name="rwkv_proj",
    )(x, x, shift, scale, *weights)


def _rwkv_kernel(r_ref, w_ref, k_ref, v_ref, a_ref, g_ref, kk_ref, ka_ref, rk_ref, lng_ref, lnb_ref,
                 o_ref, st_ref):
    C, Dh = CHUNK, RWKV_HEAD_DIM
    W2 = 2 * Dh
    pairs = range(RWKV_HEADS // 2)

    @pl.when(pl.program_id(1) == 0)
    def _init():
        st_ref[...] = jnp.zeros_like(st_ref)

    R = r_ref.shape[1]
    incl, strict, eye_p, first = _pair_masks(C)
    tri = _tri_masks(C)[0].astype(BF16)
    r2 = lax.broadcasted_iota(jnp.int32, (W2, W2), 0)
    c2 = lax.broadcasted_iota(jnp.int32, (W2, W2), 1)
    same_head = (r2 // Dh) == (c2 // Dh)
    ones_bd = same_head.astype(BF16)
    eye2 = (r2 == c2).astype(BF16)
    inv_dh = 1.0 / Dh

    units = []
    for c in range(R // C):
        rows = slice(c * C, (c + 1) * C)
        w = w_ref[0, rows, :]
        G = _tri_cumsum(tri, w)
        g_last = G[C - 1:C, :]
        e_in = jnp.exp(G)
        e_ex = jnp.exp(G - w)
        e_neg = jnp.exp(-G)
        e_rest = jnp.exp(g_last - G)
        e_last_col = jnp.broadcast_to(jnp.exp(g_last), (SUBLANES, g_last.shape[1])).T
        r = r_ref[0, rows, :].astype(F32)
        k0 = k_ref[0, rows, :].astype(F32)
        a = a_ref[0, rows, :].astype(F32)
        kk_raw = k0 * kk_ref[...]
        k_mod = k0 * (1.0 + (a - 1.0) * ka_ref[...])
        kk_slabs = [kk_raw[:, p * W2:(p + 1) * W2] for p in pairs]
        kk_sums = _head_sums([t * t for t in kk_slabs], ones_bd, split=True)
        for p in pairs:
            sl = slice(p * W2, (p + 1) * W2)
            kk = kk_slabs[p] * lax.rsqrt(kk_sums[p] + 1e-12)
            kh, rh, vh = k_mod[:, sl], r[:, sl], v_ref[0, rows, sl].astype(F32)
            bv = kk * a[:, sl]
            units.append(dict(
                c=c, sl=sl, rh=rh, kh=kh, vh=vh,
                ar=jnp.concatenate([-kk * e_ex[:, sl], rh * e_in[:, sl]], axis=0),
                b_t=bv * e_neg[:, sl], k_t=kh * e_neg[:, sl],
                bk_rest=jnp.concatenate([bv * e_rest[:, sl], kh * e_rest[:, sl]], axis=0),
                e_last=e_last_col[sl, 0:1]))
    def independent(us):
        if not us:
            return
        for u in us:
            ab = _mm_nt(u["ar"], _bd_rows(u["b_t"], first))
            ak = _mm_nt(u["ar"], _bd_rows(u["k_t"], first))
            u["L"] = -jnp.where(strict, ab[:C], 0.0)
            u["a_rb"] = jnp.where(incl, ab[C:], 0.0)
            u["akrk"] = jnp.concatenate([jnp.where(strict, ak[:C], 0.0), jnp.where(incl, ak[C:], 0.0)],
                                        axis=0)
            u["bk_rest_t"] = _mm_nt(eye2, u["bk_rest"])
        yield
        for u in us:
            u["kv"] = _mm(u["akrk"], _bd_rows(u["vh"], first))
        yield
        yield from _unit_lower_inv_pairs(us, eye_p, first)

    n_pairs = len(pairs)
    S = [st_ref[p] for p in pairs]

    def chain(chunks):
        for c in chunks:
            cu = units[c * n_pairs:(c + 1) * n_pairs]
            for p, u in zip(pairs, cu):
                u["sd"] = _mm(u["ar"], S[p]) + u["kv"]
            yield
            for u in cu:
                u["P"] = _mm(u["T"], _bd_rows(u["sd"][:C], first))
            yield
            for p, u in zip(pairs, cu):
                pv = jnp.concatenate([u["P"], u["vh"]], axis=0)
                S[p] = S[p] * u["e_last"] + jnp.where(same_head, _mm(u["bk_rest_t"], pv), 0.0)
            yield

    n_chunks = R // C
    half = max(1, n_chunks // 2)
    _emit(independent(units[:half * n_pairs]))
    _emit(independent(units[half * n_pairs:]), chain(range(half)))
    _emit(chain(range(half, n_chunks)))
    for p in pairs:
        st_ref[p] = S[p]

    for u in units:
        u["y"] = u["sd"][C:] + _mm(u["a_rb"], _bd_rows(u["P"], first))
    sums = _head_sums([u["y"] for u in units] + [u["rh"] * u["kh"] * rk_ref[:, u["sl"]] for u in units],
                      ones_bd)
    for u, s_y, s_rk in zip(units, sums[:len(units)], sums[len(units):]):
        u["yc"] = u["y"] - s_y * inv_dh
        u["bonus"] = s_rk * u["vh"]
    for u, s_var in zip(units, _head_sums([u["yc"] * u["yc"] for u in units], ones_bd)):
        sl = u["sl"]
        rows = slice(u["c"] * C, (u["c"] + 1) * C)
        yn = u["yc"] * lax.rsqrt(s_var * inv_dh + RWKV_GN_EPS) * lng_ref[:, sl] + lnb_ref[:, sl]
        o_ref[0, rows, sl] = ((yn + u["bonus"]) * g_ref[0, rows, sl].astype(F32)).astype(o_ref.dtype)


def _rwkv_call(r, w, k, v, a, g, k_k, k_a, r_k, ln_gain, ln_bias):
    B, S, D = r.shape
    H, Dh = RWKV_HEADS, RWKV_HEAD_DIM
    C = CHUNK * CHUNKS_PER_STEP
    seq = pl.BlockSpec((1, C, D), lambda b, c: (b, c, 0))
    vec = pl.BlockSpec((1, D), lambda b, c: (0, 0))
    flat = lambda t: t.reshape(1, D)
    return pl.pallas_call(
        _rwkv_kernel,
        grid=(B, S // C),
        in_specs=[seq] * 6 + [vec] * 5,
        out_specs=seq,
        out_shape=jax.ShapeDtypeStruct((B, S, D), BF16),
        scratch_shapes=[pltpu.VMEM((H // 2, 2 * Dh, 2 * Dh), F32)],
        compiler_params=_params("parallel", "arbitrary"),
        name="rwkv",
    )(r, w, k, v, a, g, flat(k_k), flat(k_a), flat(r_k), flat(ln_gain), flat(ln_bias))


def kernel(x, c, positions, w_mod, b_mod, w_in0, gdn_conv_w, gdn_a_log, gdn_dt_bias, gdn_norm_gain, mla_q_norm_gain, mla_kv_norm_gain, mla_w_uq, mla_w_ukv, mla_q_head_gain, mla_k_head_gain, w_out0, rwkv_mu, rwkv_w_r, rwkv_w_k, rwkv_w_v, rwkv_w_o, rwkv_w0, rwkv_w1, rwkv_w2, rwkv_a0, rwkv_a1, rwkv_a2, rwkv_g1, rwkv_g2, rwkv_k_k, rwkv_k_a, rwkv_r_k, rwkv_ln_gain, rwkv_ln_bias, w_up, w_down):
    B, S, D = x.shape
    mod = _mod_call(c, w_mod, b_mod)

    def mod_chunks(layer):
        return [mod[layer, :, i * D:(i + 1) * D].reshape(B, 1, D) for i in range(6)]

    shift1, scale1, gate1, shift2, scale2, gate2 = mod_chunks(0)
    qkvz, lat = _inproj_call(x, shift1, scale1, _pack_w_in(w_in0[0]))
    o_gdn = _gdn_call(qkvz, lat, gdn_conv_w[0], gdn_a_log[0], gdn_dt_bias[0], gdn_norm_gain[0])
    q, k, v = _mla_prep_call(lat, positions, mla_q_norm_gain[0], mla_kv_norm_gain[0], mla_w_uq[0],
                             mla_w_ukv[0], mla_q_head_gain[0], mla_k_head_gain[0])
    o_mla = _flash_call(q, k, v)
    x = _mix_mlp_call([o_gdn, o_mla], x, gate1, shift2, scale2, gate2, w_out0[0].astype(BF16),
                      w_up[0].astype(BF16), w_down[0].astype(BF16))

    shift1, scale1, gate1, shift2, scale2, gate2 = mod_chunks(1)
    r, w, k, v, a, g = _rwkv_proj_call(x, shift1, scale1, rwkv_mu[0], rwkv_w_r[0], rwkv_w_k[0],
                                       rwkv_w_v[0], rwkv_w1[0], rwkv_w2[0], rwkv_a1[0], rwkv_a2[0],
                                       rwkv_g1[0], rwkv_g2[0], rwkv_w0[0], rwkv_a0[0])
    y = _rwkv_call(r, w, k, v, a, g, rwkv_k_k[0], rwkv_k_a[0], rwkv_r_k[0], rwkv_ln_gain[0],
                   rwkv_ln_bias[0])
    x = _mix_mlp_call([y], x, gate1, shift2, scale2, gate2, rwkv_w_o[0].astype(BF16),
                      w_up[1].astype(BF16), w_down[1].astype(BF16))
    return x
```

```python
import functools
import math

import jax
import jax.numpy as jnp
from jax import lax
from jax.experimental import pallas as pl
from jax.experimental.pallas import tpu as pltpu

F32 = jnp.float32
BF16 = jnp.bfloat16

D_MODEL = 1024
DEPTH = 2
GDN_HEADS = 8
GDN_HEAD_DIM = 64
GDN_WIDTH = GDN_HEADS * GDN_HEAD_DIM
GDN_CONV = 4
CHUNK = 64
GDN_CHUNKS_PER_STEP = 8
RWKV_CHUNKS_PER_STEP = 4
FLASH_TQ = 1024
FLASH_TK = 256
FLASH_UNROLL = (8, 4, 1)
MLA_HEADS = 8
MLA_Q_RANK = 256
MLA_KV_RANK = 128
MLA_NOPE = 64
MLA_ROPE = 32
MLA_QK = MLA_NOPE + MLA_ROPE
MLA_V = 64
ROPE_THETA = 10000.0
NEG_INF = -1e30
RWKV_HEADS = 16
RWKV_HEAD_DIM = 64
RWKV_GN_EPS = 64e-5
D_FF = 4 * D_MODEL
EPS = 1e-6
LANES = 128
SUBLANES = 8
VMEM_LIMIT = 56 * 1024 * 1024


def _params(*sem):
    return pltpu.CompilerParams(dimension_semantics=sem, vmem_limit_bytes=VMEM_LIMIT)


def _mm(a, b):
    return jnp.dot(a.astype(BF16), b.astype(BF16), preferred_element_type=F32)


def _mm_nt(a, b):
    return lax.dot_general(a.astype(BF16), b.astype(BF16), (((1,), (1,)), ((), ())),
                           preferred_element_type=F32)


def _tri_cumsum(tri_bf16, x):
    hi = x.astype(BF16)
    r1 = x - hi.astype(F32)
    mid = r1.astype(BF16)
    lo = (r1 - mid.astype(F32)).astype(BF16)
    dot = functools.partial(jnp.dot, preferred_element_type=F32)
    return dot(tri_bf16, hi) + dot(tri_bf16, mid) + dot(tri_bf16, lo)


def _unit_lower_inv(Ls, eye):
    n = Ls[0].shape[0]
    row = lax.broadcasted_iota(jnp.int32, (n, n), 0)
    col = lax.broadcasted_iota(jnp.int32, (n, n), 1)
    Ts = [eye] * len(Ls)
    s = 1
    while s < n:
        lower_left = ((row // (2 * s)) == (col // (2 * s))) & ((row // s) % 2 == 1) & ((col // s) % 2 == 0)
        Cs = [jnp.where(lower_left, L, 0.0) for L in Ls]
        if s == 1:
            Ts = [T - C for T, C in zip(Ts, Cs)]
        else:
            Xs = [_mm(T, C) for T, C in zip(Ts, Cs)]
            Ts = [T - _mm(X, T) for T, X in zip(Ts, Xs)]
        s *= 2
    return Ts


def _pair_masks(n):
    row = lax.broadcasted_iota(jnp.int32, (n, 2 * n), 0)
    lane = lax.broadcasted_iota(jnp.int32, (n, 2 * n), 1)
    col = lane % n
    return row >= col, row > col, (row == col).astype(F32), lane < n


def _bd_rows(x, first):
    x = x.astype(BF16)
    zero = jnp.zeros_like(x)
    return jnp.concatenate([jnp.where(first, x, zero), jnp.where(first, zero, x)], axis=0)


def _unit_lower_inv_pairs(units, eye_p, first):
    n = units[0]["L"].shape[0]
    row = lax.broadcasted_iota(jnp.int32, (n, 2 * n), 0)
    col = lax.broadcasted_iota(jnp.int32, (n, 2 * n), 1) % n
    Ts = [eye_p] * len(units)
    s = 1
    while s < n:
        lower_left = ((row // (2 * s)) == (col // (2 * s))) & ((row // s) % 2 == 1) & ((col // s) % 2 == 0)
        Cs = [jnp.where(lower_left, u["L"], 0.0) for u in units]
        if s == 1:
            Ts = [T - C for T, C in zip(Ts, Cs)]
        else:
            Xs = [_mm(T, _bd_rows(C, first)) for T, C in zip(Ts, Cs)]
            yield
            Ts = [T - _mm(X, _bd_rows(T, first)) for T, X in zip(Ts, Xs)]
            yield
        s *= 2
    for u, T in zip(units, Ts):
        u["T"] = T


def _emit(*gens):
    live = list(gens)
    while live:
        for g in list(live):
            try:
                next(g)
            except StopIteration:
                live.remove(g)


def _head_sums(xs, ones_bd, split=False):
    n = xs[0].shape[0]
    x = jnp.concatenate(xs, axis=0)
    hi = x.astype(BF16)
    out = jnp.dot(hi, ones_bd, preferred_element_type=F32)
    if split:
        out = out + jnp.dot((x - hi.astype(F32)).astype(BF16), ones_bd, preferred_element_type=F32)
    return [out[i * n:(i + 1) * n] for i in range(len(xs))]


def _softplus(x):
    return jnp.maximum(x, 0.0) + jnp.log1p(jnp.exp(-jnp.abs(x)))


def _silu(x):
    return x * jax.nn.sigmoid(x)


def _adaln(x, shift, scale):
    ms = jnp.mean(x * x, axis=-1, keepdims=True)
    return x * lax.rsqrt(ms + EPS) * (1.0 + scale) + shift


def _tri_masks(n):
    row = lax.broadcasted_iota(jnp.int32, (n, n), 0)
    col = lax.broadcasted_iota(jnp.int32, (n, n), 1)
    return row >= col, row > col, (row == col).astype(F32)


def _mod_kernel(c_ref, w_ref, b_ref, o_ref):
    cond = _silu(c_ref[...])
    o_ref[0] = jnp.dot(cond, w_ref[0], precision=lax.Precision.HIGHEST,
                       preferred_element_type=F32) + b_ref[0]


def _mod_call(c, w_mod, b_mod):
    B, D = c.shape
    depth, _, n = w_mod.shape
    tn = 1536
    c_pad = jnp.pad(c, ((0, SUBLANES - B), (0, 0)))
    out = pl.pallas_call(
        _mod_kernel,
        grid=(depth, n // tn),
        in_specs=[pl.BlockSpec((SUBLANES, D), lambda l, j: (0, 0)),
                  pl.BlockSpec((1, D, tn), lambda l, j: (l, 0, j)),
                  pl.BlockSpec((1, 1, tn), lambda l, j: (l, 0, j))],
        out_specs=pl.BlockSpec((1, SUBLANES, tn), lambda l, j: (l, 0, j)),
        out_shape=jax.ShapeDtypeStruct((depth, SUBLANES, n), F32),
        compiler_params=_params("parallel", "parallel"),
        name="mod",
    )(c_pad, w_mod, b_mod.reshape(depth, 1, n))
    return out[:, :B, :]


IN_PACKED = 2560


def _pack_w_in(w_in):
    D = w_in.shape[0]
    o_a = 4 * GDN_WIDTH
    o_dq = o_a + 2 * GDN_HEADS
    o_dkv = o_dq + MLA_Q_RANK
    o_kr = o_dkv + MLA_KV_RANK
    z = lambda n: jnp.zeros((D, n), w_in.dtype)
    return jnp.concatenate(
        [w_in[:, :o_a], w_in[:, o_dq:o_dkv], w_in[:, o_dkv:o_kr],
         w_in[:, o_a:o_dq], z(MLA_NOPE - 2 * GDN_HEADS), w_in[:, o_kr:o_kr + MLA_ROPE],
         z(LANES - MLA_NOPE - MLA_ROPE)], axis=1).astype(BF16)


def _inproj_kernel(x_ref, sh_ref, sc_ref, w_ref, qkvz_ref, lat_ref):
    h = _adaln(x_ref[0], sh_ref[0], sc_ref[0]).astype(BF16)
    y = jnp.dot(h, w_ref[...], preferred_element_type=F32)
    qkvz_ref[0] = y[:, :4 * GDN_WIDTH].astype(qkvz_ref.dtype)
    lat_ref[0] = y[:, 4 * GDN_WIDTH:]


def _inproj_call(x, shift, scale, w_packed):
    B, S, D = x.shape
    tm = min(1024, S)
    n_lat = IN_PACKED - 4 * GDN_WIDTH
    return pl.pallas_call(
        _inproj_kernel,
        grid=(B, S // tm),
        in_specs=[pl.BlockSpec((1, tm, D), lambda b, i: (b, i, 0)),
                  pl.BlockSpec((1, 1, D), lambda b, i: (b, 0, 0)),
                  pl.BlockSpec((1, 1, D), lambda b, i: (b, 0, 0)),
                  pl.BlockSpec((D, IN_PACKED), lambda b, i: (0, 0))],
        out_specs=[pl.BlockSpec((1, tm, 4 * GDN_WIDTH), lambda b, i: (b, i, 0)),
                   pl.BlockSpec((1, tm, n_lat), lambda b, i: (b, i, 0))],
        out_shape=[jax.ShapeDtypeStruct((B, S, 4 * GDN_WIDTH), BF16),
                   jax.ShapeDtypeStruct((B, S, n_lat), F32)],
        compiler_params=_params("parallel", "parallel"),
        name="inproj",
    )(x, shift, scale, w_packed)


def _gdn_kernel(x_ref, ab_ref, cw_ref, alog_ref, dtb_ref, gain_ref, o_ref, ext_ref, st_ref):
    C, H, Dh, W = CHUNK, GDN_HEADS, GDN_HEAD_DIM, GDN_WIDTH

    @pl.when(pl.program_id(1) == 0)
    def _init():
        ext_ref[0:SUBLANES, :] = jnp.zeros((SUBLANES, 3 * W), F32)
        st_ref[...] = jnp.zeros_like(st_ref)

    R = x_ref.shape[1]
    x = x_ref[0].astype(F32)
    ext_ref[SUBLANES:SUBLANES + R, :] = x[:, :3 * W]
    base = SUBLANES - (GDN_CONV - 1)
    conv = cw_ref[0:1, :] * ext_ref[base:base + R, :]
    for j in range(1, GDN_CONV):
        conv = conv + cw_ref[j:j + 1, :] * ext_ref[base + j:base + j + R, :]
    ext_ref[0:SUBLANES, :] = x[R - SUBLANES:R, :3 * W]
    qkv = _silu(conv)

    W2 = 2 * Dh
    pairs = range(H // 2)
    causal, strict, eye_p, first = _pair_masks(C)
    tri = _tri_masks(C)[0].astype(BF16)
    r2 = lax.broadcasted_iota(jnp.int32, (W2, W2), 0)
    c2 = lax.broadcasted_iota(jnp.int32, (W2, W2), 1)
    same_head = (r2 // Dh) == (c2 // Dh)
    ones_bd = same_head.astype(BF16)
    eye2 = (r2 == c2).astype(BF16)
    first_row = first[0:1, :]
    ab = ab_ref[0]
    beta_all = jax.nn.sigmoid(ab)
    g_all = -jnp.exp(alog_ref[...]) * _softplus(ab + dtb_ref[...])
    qk_slabs = [qkv[:, off + p * W2:off + (p + 1) * W2] for off in (0, W) for p in pairs]
    qk_norm = [t * lax.rsqrt(ss + 1e-12)
               for t, ss in zip(qk_slabs, _head_sums([t * t for t in qk_slabs], ones_bd, split=True))]
    q_n, k_n = qk_norm[:len(pairs)], qk_norm[len(pairs):]

    units = []
    for c in range(R // C):
        rows = slice(c * C, (c + 1) * C)
        gcum = _tri_cumsum(tri, g_all[rows])
        gcum_t = gcum.T
        eg = jnp.exp(gcum)
        glast = gcum[C - 1:C, :]
        eg_rest = jnp.exp(glast - gcum)
        eg_last = jnp.exp(glast)
        for p in pairs:
            h0, h1 = 2 * p, 2 * p + 1
            per_head = lambda t, off=0: jnp.where(first, t[:, off + h0:off + h0 + 1], t[:, off + h1:off + h1 + 1])
            q = q_n[p][rows] * (Dh ** -0.5)
            k = k_n[p][rows]
            bc = per_head(beta_all[rows], H)
            eg_p = per_head(eg)
            kb = k * bc
            g_col = jnp.concatenate([gcum_t[h0:h0 + 1, :], gcum_t[h1:h1 + 1, :]], axis=1)
            decay = jnp.where(causal, jnp.exp(jnp.where(causal, per_head(gcum) - g_col, 0.0)), 0.0)
            units.append(dict(
                c=c, p=p, decay=decay, vb=qkv[rows, 2 * W + p * W2:2 * W + (p + 1) * W2] * bc, k=k,
                kbq=jnp.concatenate([kb, q], axis=0),
                kbq_g=jnp.concatenate([kb * eg_p, q * eg_p], axis=0),
                k_rest=k * per_head(eg_rest),
                e_last=jnp.where(first_row, eg_last[:, h0:h0 + 1], eg_last[:, h1:h1 + 1])))
    def independent(us):
        if not us:
            return
        for u in us:
            kk = _mm_nt(u["kbq"], _bd_rows(u["k"], first))
            u["L"] = jnp.where(strict, kk[:C] * u["decay"], 0.0)
            u["qk"] = jnp.where(causal, kk[C:] * u["decay"], 0.0)
            u["k_rest_t"] = _mm_nt(eye2, u["k_rest"])
        yield
        yield from _unit_lower_inv_pairs(us, eye_p, first)

    n_pairs = len(pairs)
    S = [st_ref[p] for p in pairs]

    def chain(chunks):
        for c in chunks:
            cu = units[c * n_pairs:(c + 1) * n_pairs]
            for u in cu:
                u["ks"] = _mm(u["kbq_g"], S[u["p"]])
            yield
            for u in cu:
                u["v_new"] = _mm(u["T"], _bd_rows(u["vb"] - u["ks"][:C], first))
            yield
            for u in cu:
                S[u["p"]] = (S[u["p"]] * u["e_last"]
                             + jnp.where(same_head, _mm(u["k_rest_t"], u["v_new"]), 0.0))
            yield

    n_chunks = R // C
    half = max(1, n_chunks // 2)
    _emit(independent(units[:half * n_pairs]))
    _emit(independent(units[half * n_pairs:]), chain(range(half)))
    _emit(chain(range(half, n_chunks)))
    for p in pairs:
        st_ref[p] = S[p]

    for u in units:
        u["o"] = u["ks"][C:] + _mm(u["qk"], _bd_rows(u["v_new"], first))
    for u, ss in zip(units, _head_sums([u["o"] * u["o"] for u in units], ones_bd)):
        rows = slice(u["c"] * C, (u["c"] + 1) * C)
        sl = slice(u["p"] * W2, (u["p"] + 1) * W2)
        o = u["o"] * lax.rsqrt(ss * (1.0 / Dh) + EPS)
        z = x[rows, 3 * W + u["p"] * W2:3 * W + (u["p"] + 1) * W2]
        o_ref[0, rows, sl] = (o * gain_ref[...] * _silu(z)).astype(o_ref.dtype)


def _gdn_call(qkvz, lat, conv_w, a_log, dt_bias, gain):
    B, S, _ = qkvz.shape
    H, Dh, W = GDN_HEADS, GDN_HEAD_DIM, GDN_WIDTH
    C = CHUNK * GDN_CHUNKS_PER_STEP
    lane_pad = lambda t: jnp.pad(t.reshape(1, H), ((0, 0), (0, LANES - H)))
    ab_block = (lat.shape[-1] // LANES) - 1
    return pl.pallas_call(
        _gdn_kernel,
        grid=(B, S // C),
        in_specs=[pl.BlockSpec((1, C, 4 * W), lambda b, c: (b, c, 0)),
                  pl.BlockSpec((1, C, LANES), lambda b, c: (b, c, ab_block)),
                  pl.BlockSpec((GDN_CONV, 3 * W), lambda b, c: (0, 0)),
                  pl.BlockSpec((1, LANES), lambda b, c: (0, 0)),
                  pl.BlockSpec((1, LANES), lambda b, c: (0, 0)),
                  pl.BlockSpec((1, 2 * Dh), lambda b, c: (0, 0))],
        out_specs=pl.BlockSpec((1, C, W), lambda b, c: (b, c, 0)),
        out_shape=jax.ShapeDtypeStruct((B, S, W), BF16),
        scratch_shapes=[pltpu.VMEM((C + SUBLANES, 3 * W), F32),
                        pltpu.VMEM((H // 2, 2 * Dh, 2 * Dh), F32)],
        compiler_params=_params("parallel", "arbitrary"),
        name="gdn",
    )(qkvz, lat, conv_w, lane_pad(a_log), lane_pad(dt_bias), jnp.tile(gain.reshape(1, Dh), (1, 2)))


def _mla_prep_kernel(lat_ref, pos_ref, invf_ref, qg_ref, kvg_ref, wuq_ref, wuk_ref, wuv_ref,
                     qhg_ref, khg_ref, q_ref, k_ref, v_ref):
    lat = lat_ref[0]
    dq = lat[:, :MLA_Q_RANK]
    dkv = lat[:, MLA_Q_RANK:MLA_Q_RANK + MLA_KV_RANK]
    slab = lat[:, MLA_Q_RANK + MLA_KV_RANK:]
    q_lat = dq * lax.rsqrt(jnp.mean(dq * dq, axis=-1, keepdims=True) + EPS) * qg_ref[...]
    kv_lat = dkv * lax.rsqrt(jnp.mean(dkv * dkv, axis=-1, keepdims=True) + EPS) * kvg_ref[...]
    qm = _mm(q_lat, wuq_ref[...])
    kn = _mm(kv_lat, wuk_ref[...])
    v_row = lax.broadcasted_iota(jnp.int32, (MLA_HEADS * LANES, 1), 0)
    v_ones = ((v_row % LANES) >= MLA_V).astype(F32)
    v_ref[0] = (_mm_nt(wuv_ref[...], kv_lat) + v_ones).astype(BF16)

    lane = lax.broadcasted_iota(jnp.int32, slab.shape, 1)
    half = MLA_ROPE // 2
    ang = pos_ref[0] * invf_ref[...]
    cosf = jnp.cos(ang)
    sinf = jnp.sin(ang)
    sin_signed = jnp.where(lane < MLA_NOPE + half, -sinf, sinf)
    k_rope = jnp.where((lane >= MLA_NOPE) & (lane < MLA_QK), slab, 0.0)

    def head_norm_rope(t, gain):
        t = t * lax.rsqrt(jnp.sum(t * t, axis=-1, keepdims=True) * (1.0 / MLA_QK) + EPS) * gain
        partner = jnp.where(lane < MLA_NOPE + half,
                            pltpu.roll(t, LANES - half, 1), pltpu.roll(t, half, 1))
        return t * cosf + partner * sin_signed

    scale = MLA_QK ** -0.5 * math.log2(math.e)
    for h in range(MLA_HEADS):
        sl = slice(h * LANES, (h + 1) * LANES)
        q_ref[0, :, sl] = (head_norm_rope(qm[:, sl], qhg_ref[...]) * scale).astype(BF16)
        k_ref[0, :, sl] = head_norm_rope(kn[:, sl] + k_rope, khg_ref[...]).astype(BF16)


def _mla_prep_call(lat, positions, q_norm_gain, kv_norm_gain, w_uq, w_ukv, q_head_gain, k_head_gain):
    B, S, n_lat = lat.shape
    H = MLA_HEADS
    tm = min(512, S)
    wuq = jnp.pad(w_uq.reshape(MLA_Q_RANK, H, MLA_QK), ((0, 0), (0, 0), (0, LANES - MLA_QK)))
    wuq = wuq.reshape(MLA_Q_RANK, H * LANES).astype(BF16)
    wkv = w_ukv.reshape(MLA_KV_RANK, H, MLA_NOPE + MLA_V)
    wuk = jnp.pad(wkv[:, :, :MLA_NOPE], ((0, 0), (0, 0), (0, LANES - MLA_NOPE)))
    wuk = wuk.reshape(MLA_KV_RANK, H * LANES).astype(BF16)
    wuv = jnp.pad(wkv[:, :, MLA_NOPE:], ((0, 0), (0, 0), (0, LANES - MLA_V)))
    wuv = wuv.reshape(MLA_KV_RANK, H * LANES).T.astype(BF16)
    pad_gain = lambda g: jnp.pad(g.reshape(1, MLA_QK), ((0, 0), (0, LANES - MLA_QK)))
    inv_freq = ROPE_THETA ** (-jnp.arange(0, MLA_ROPE, 2, dtype=F32) / MLA_ROPE)
    invf = jnp.concatenate([jnp.zeros((MLA_NOPE,), F32), inv_freq, inv_freq,
                            jnp.zeros((LANES - MLA_QK,), F32)]).reshape(1, LANES)
    pos = positions.astype(F32).reshape(B, S, 1)
    const = lambda shape: pl.BlockSpec(shape, lambda b, i: (0,) * len(shape))
    return pl.pallas_call(
        _mla_prep_kernel,
        grid=(B, S // tm),
        in_specs=[pl.BlockSpec((1, tm, n_lat), lambda b, i: (b, i, 0)),
                  pl.BlockSpec((1, tm, 1), lambda b, i: (b, i, 0)),
                  const((1, LANES)), const((1, MLA_Q_RANK)), const((1, MLA_KV_RANK)),
                  const((MLA_Q_RANK, H * LANES)), const((MLA_KV_RANK, H * LANES)),
                  const((H * LANES, MLA_KV_RANK)), const((1, LANES)), const((1, LANES))],
        out_specs=[pl.BlockSpec((1, tm, H * LANES), lambda b, i: (b, i, 0)),
                   pl.BlockSpec((1, tm, H * LANES), lambda b, i: (b, i, 0)),
                   pl.BlockSpec((1, H * LANES, tm), lambda b, i: (b, 0, i))],
        out_shape=[jax.ShapeDtypeStruct((B, S, H * LANES), BF16)] * 2
        + [jax.ShapeDtypeStruct((B, H * LANES, S), BF16)],
        compiler_params=_params("parallel", "parallel"),
        name="mla_prep",
    )(lat, pos, invf, q_norm_gain.reshape(1, -1), kv_norm_gain.reshape(1, -1), wuq, wuk, wuv,
      pad_gain(q_head_gain), pad_gain(k_head_gain))


def _flash_kernel(q_ref, k_ref, vt_ref, o_ref, *, tq, tk):
    qi = pl.program_id(2)
    q = q_ref[0]
    q0, q1 = q[:, :LANES], q[:, LANES:]
    key = lax.broadcasted_iota(jnp.int32, (tk, tq), 0)
    qry = lax.broadcasted_iota(jnp.int32, (tk, tq), 1)

    def scores(kj, c0=0):
        start = pl.multiple_of(kj * tk, tk)
        rows = pl.ds(start, tk)
        return (_mm_nt(k_ref[0, rows, :LANES], q0[c0:]),
                _mm_nt(k_ref[0, rows, LANES:], q1[c0:]))

    def online(st, m, acc, vt):
        m_new = jnp.maximum(m, jnp.max(st, axis=0, keepdims=True))
        pt = jnp.exp2(st - m_new)
        return m_new, jnp.exp2(m - m_new) * acc + _mm(vt, pt)

    def softmax_pv(kj, s0, s1, m0, m1, acc0, acc1):
        start = pl.multiple_of(kj * tk, tk)
        cols = pl.ds(start, tk)
        m0, acc0 = online(s0, m0, acc0, vt_ref[0, :LANES, cols])
        m1, acc1 = online(s1, m1, acc1, vt_ref[0, LANES:, cols])
        return m0, m1, acc0, acc1

    def group(size, first_block):
        def body(t, carry):
            blocks = [first_block + size * t + u for u in range(size)]
            ss = [scores(kj) for kj in blocks]
            for kj, s in zip(blocks, ss):
                carry = softmax_pv(kj, *s, *carry)
            return carry
        return body

    neg = jnp.full((1, tq), NEG_INF, F32)
    zero = jnp.zeros((LANES, tq), F32)
    n_full = qi * (tq // tk)
    carry = (neg, neg, zero, zero)
    done = 0
    for size in FLASH_UNROLL:
        trips = (n_full - done) // size
        carry = lax.fori_loop(0, trips, group(size, done), carry)
        done = done + trips * size
    m0, m1, acc0, acc1 = carry
    for d in range(tq // tk):
        c0 = d * tk
        s0, s1 = scores(n_full + d, c0)
        mask = (key <= qry)[:, :tq - c0]
        sub = softmax_pv(n_full + d, jnp.where(mask, s0, NEG_INF), jnp.where(mask, s1, NEG_INF),
                         m0[:, c0:], m1[:, c0:], acc0[:, c0:], acc1[:, c0:])
        m0, m1, acc0, acc1 = (jnp.concatenate([old[:, :c0], new], axis=1) if c0 else new
                              for old, new in zip((m0, m1, acc0, acc1), sub))
    out_t = jnp.concatenate([acc0[:MLA_V] / acc0[MLA_V:MLA_V + 1],
                             acc1[:MLA_V] / acc1[MLA_V:MLA_V + 1]], axis=0)
    o_ref[0] = out_t.T.astype(o_ref.dtype)


def _flash_call(q, k, v):
    B, S, _ = q.shape
    tq = min(FLASH_TQ, S)
    pairs = MLA_HEADS // 2
    return pl.pallas_call(
        functools.partial(_flash_kernel, tq=tq, tk=min(FLASH_TK, tq)),
        grid=(B, pairs, S // tq),
        in_specs=[pl.BlockSpec((1, tq, 2 * LANES), lambda b, p, i: (b, i, p)),
                  pl.BlockSpec((1, S, 2 * LANES), lambda b, p, i: (b, 0, p)),
                  pl.BlockSpec((1, 2 * LANES, S), lambda b, p, i: (b, p, 0))],
        out_specs=pl.BlockSpec((1, tq, 2 * MLA_V), lambda b, p, i: (b, i, p)),
        out_shape=jax.ShapeDtypeStruct((B, S, MLA_HEADS * MLA_V), BF16),
        compiler_params=_params("parallel", "parallel", "arbitrary"),
        name="flash",
    )(q, k, v)


def _mix_mlp_kernel(*refs, n_in):
    ins = refs[:n_in]
    (x_ref, g1_ref, sh_ref, sc_ref, g2_ref, wo_ref, wu_ref, wd_ref, o_ref,
     x1_ref, h_ref, acc_ref) = refs[n_in:]
    f = pl.program_id(2)

    @pl.when(f == 0)
    def _first():
        y = None
        off = 0
        for r in ins:
            n = r.shape[-1]
            part = _mm(r[0], wo_ref[off:off + n, :])
            y = part if y is None else y + part
            off += n
        x1 = x_ref[0] + g1_ref[0] * y
        x1_ref[...] = x1
        h_ref[...] = _adaln(x1, sh_ref[0], sc_ref[0]).astype(BF16)
        acc_ref[...] = jnp.zeros_like(acc_ref)

    u = jnp.maximum(jnp.dot(h_ref[...], wu_ref[...], preferred_element_type=F32), 0.0)
    acc_ref[...] += _mm(u * u, wd_ref[...])

    @pl.when(f == pl.num_programs(2) - 1)
    def _last():
        o_ref[0] = x1_ref[...] + g2_ref[0] * acc_ref[...]


def _mix_mlp_call(ins, x, gate1, shift, scale, gate2, w_out, w_up, w_down):
    B, S, D = x.shape
    F = w_up.shape[1]
    tm = min(1024, S)
    tf = 1024
    n_in = len(ins)
    vec = pl.BlockSpec((1, 1, D), lambda b, i, f: (b, 0, 0))
    rows = lambda n: pl.BlockSpec((1, tm, n), lambda b, i, f: (b, i, 0))
    return pl.pallas_call(
        functools.partial(_mix_mlp_kernel, n_in=n_in),
        grid=(B, S // tm, F // tf),
        in_specs=[rows(a.shape[-1]) for a in ins] + [rows(D), vec, vec, vec, vec,
                  pl.BlockSpec(w_out.shape, lambda b, i, f: (0, 0)),
                  pl.BlockSpec((D, tf), lambda b, i, f: (0, f)),
                  pl.BlockSpec((tf, D), lambda b, i, f: (f, 0))],
        out_specs=rows(D),
        out_shape=jax.ShapeDtypeStruct((B, S, D), F32),
        scratch_shapes=[pltpu.VMEM((tm, D), F32), pltpu.VMEM((tm, D), BF16), pltpu.VMEM((tm, D), F32)],
        compiler_params=_params("parallel", "parallel", "arbitrary"),
        name="mix_mlp",
    )(*ins, x, gate1, shift, scale, gate2, w_out, w_up, w_down)


def _rwkv_proj_kernel(x_ref, xp_ref, sh_ref, sc_ref, mu_ref, wr_ref, wk_ref, wv_ref,
                      w1_ref, w2_ref, a1_ref, a2_ref, g1_ref, g2_ref, w0_ref, a0_ref,
                      r_ref, w_ref, k_ref, v_ref, a_ref, g_ref):
    h = _adaln(x_ref[0], sh_ref[0], sc_ref[0])
    tm = h.shape[0]
    prev_tail = _adaln(xp_ref[0], sh_ref[0], sc_ref[0])[SUBLANES - 1:SUBLANES, :]
    prev_tail = jnp.where(pl.program_id(1) == 0, 0.0, prev_tail)
    row = lax.broadcasted_iota(jnp.int32, h.shape, 0)
    h_prev = jnp.where(row == 0, prev_tail, pltpu.roll(h, 1, 0))
    xx = h_prev - h
    mix = lambda j: h + xx * mu_ref[j:j + 1, :]
    xr, xw, xk, xv, xa, xg = (mix(j) for j in range(6))
    r_ref[0] = _mm(xr, wr_ref[...]).astype(r_ref.dtype)
    k_ref[0] = _mm(xk, wk_ref[...]).astype(k_ref.dtype)
    v_ref[0] = _mm(xv, wv_ref[...]).astype(v_ref.dtype)
    wl = w0_ref[...] + _mm(jnp.tanh(_mm(xw, w1_ref[...])), w2_ref[...])
    w_log = -_softplus(-wl) - 0.5
    w_ref[0] = -jnp.exp(w_log)
    a_ref[0] = jax.nn.sigmoid(a0_ref[...] + _mm(_mm(xa, a1_ref[...]), a2_ref[...])).astype(a_ref.dtype)
    g_ref[0] = _mm(jax.nn.sigmoid(_mm(xg, g1_ref[...])), g2_ref[...]).astype(g_ref.dtype)


def _pad_lora(w_a, w_b):
    r = w_a.shape[1]
    rp = -(-r // LANES) * LANES
    return (jnp.pad(w_a, ((0, 0), (0, rp - r))).astype(BF16),
            jnp.pad(w_b, ((0, rp - r), (0, 0))).astype(BF16))


def _rwkv_proj_call(x, shift, scale, mu, w_r, w_k, w_v, w1, w2, a1, a2, g1, g2, w0, a0):
    B, S, D = x.shape
    tm = min(512, S)
    w1p, w2p = _pad_lora(w1, w2)
    a1p, a2p = _pad_lora(a1, a2)
    g1p, g2p = _pad_lora(g1, g2)
    const = lambda t: pl.BlockSpec(t.shape, lambda b, i: (0,) * t.ndim)
    vec = pl.BlockSpec((1, 1, D), lambda b, i: (b, 0, 0))
    row_blocks = tm // SUBLANES
    weights = (mu, w_r.astype(BF16), w_k.astype(BF16), w_v.astype(BF16), w1p, w2p, a1p, a2p,
               g1p, g2p, w0.reshape(1, D), a0.reshape(1, D))
    out = pl.BlockSpec((1, tm, D), lambda b, i: (b, i, 0))
    return pl.pallas_call(
        _rwkv_proj_kernel,
        grid=(B, S // tm),
        in_specs=[pl.BlockSpec((1, tm, D), lambda b, i: (b, i, 0)),
                  pl.BlockSpec((1, SUBLANES, D),
                               lambda b, i: (b, jnp.maximum(i * row_blocks - 1, 0), 0)),
                  vec, vec] + [const(t) for t in weights],
        out_specs=[out] * 6,
        out_shape=[jax.ShapeDtypeStruct((B, S, D), F32 if i == 1 else BF16) for i in range(6)],
        compiler_params=_params("parallel", "parallel"),
        name="rwkv_proj",
    )(x, x, shift, scale, *weights)


def _rwkv_kernel(r_ref, w_ref, k_ref, v_ref, a_ref, g_ref, kk_ref, ka_ref, rk_ref, lng_ref, lnb_ref,
                 o_ref, st_ref):
    C, Dh = CHUNK, RWKV_HEAD_DIM
    W2 = 2 * Dh
    pairs = range(RWKV_HEADS // 2)

    @pl.when(pl.program_id(1) == 0)
    def _init():
        st_ref[...] = jnp.zeros_like(st_ref)

    R = r_ref.shape[1]
    incl, strict, eye_p, first = _pair_masks(C)
    tri = _tri_masks(C)[0].astype(BF16)
    r2 = lax.broadcasted_iota(jnp.int32, (W2, W2), 0)
    c2 = lax.broadcasted_iota(jnp.int32, (W2, W2), 1)
    same_head = (r2 // Dh) == (c2 // Dh)
    ones_bd = same_head.astype(BF16)
    eye2 = (r2 == c2).astype(BF16)
    inv_dh = 1.0 / Dh

    units = []
    for c in range(R // C):
        rows = slice(c * C, (c + 1) * C)
        w = w_ref[0, rows, :]
        G = _tri_cumsum(tri, w)
        g_last = G[C - 1:C, :]
        e_in = jnp.exp(G)
        e_ex = jnp.exp(G - w)
        e_neg = jnp.exp(-G)
        e_rest = jnp.exp(g_last - G)
        e_last_col = jnp.broadcast_to(jnp.exp(g_last), (SUBLANES, g_last.shape[1])).T
        r = r_ref[0, rows, :].astype(F32)
        k0 = k_ref[0, rows, :].astype(F32)
        a = a_ref[0, rows, :].astype(F32)
        kk_raw = k0 * kk_ref[...]
        k_mod = k0 * (1.0 + (a - 1.0) * ka_ref[...])
        kk_slabs = [kk_raw[:, p * W2:(p + 1) * W2] for p in pairs]
        kk_sums = _head_sums([t * t for t in kk_slabs], ones_bd, split=True)
        for p in pairs:
            sl = slice(p * W2, (p + 1) * W2)
            kk = kk_slabs[p] * lax.rsqrt(kk_sums[p] + 1e-12)
            kh, rh, vh = k_mod[:, sl], r[:, sl], v_ref[0, rows, sl].astype(F32)
            bv = kk * a[:, sl]
            units.append(dict(
                c=c, sl=sl, rh=rh, kh=kh, vh=vh,
                ar=jnp.concatenate([-kk * e_ex[:, sl], rh * e_in[:, sl]], axis=0),
                b_t=bv * e_neg[:, sl], k_t=kh * e_neg[:, sl],
                bk_rest=jnp.concatenate([bv * e_rest[:, sl], kh * e_rest[:, sl]], axis=0),
                e_last=e_last_col[sl, 0:1]))
    def independent(us):
        if not us:
            return
        for u in us:
            ab = _mm_nt(u["ar"], _bd_rows(u["b_t"], first))
            ak = _mm_nt(u["ar"], _bd_rows(u["k_t"], first))
            u["L"] = -jnp.where(strict, ab[:C], 0.0)
            u["a_rb"] = jnp.where(incl, ab[C:], 0.0)
            u["akrk"] = jnp.concatenate([jnp.where(strict, ak[:C], 0.0), jnp.where(incl, ak[C:], 0.0)],
                                        axis=0)
            u["bk_rest_t"] = _mm_nt(eye2, u["bk_rest"])
        yield
        for u in us:
            u["kv"] = _mm(u["akrk"], _bd_rows(u["vh"], first))
        yield
        yield from _unit_lower_inv_pairs(us, eye_p, first)

    n_pairs = len(pairs)
    S = [st_ref[p] for p in pairs]

    def chain(chunks):
        for c in chunks:
            cu = units[c * n_pairs:(c + 1) * n_pairs]
            for p, u in zip(pairs, cu):
                u["sd"] = _mm(u["ar"], S[p]) + u["kv"]
            yield
            for u in cu:
                u["P"] = _mm(u["T"], _bd_rows(u["sd"][:C], first))
            yield
            for p, u in zip(pairs, cu):
                pv = jnp.concatenate([u["P"], u["vh"]], axis=0)
                S[p] = S[p] * u["e_last"] + jnp.where(same_head, _mm(u["bk_rest_t"], pv), 0.0)
            yield

    n_chunks = R // C
    half = max(1, n_chunks // 2)
    _emit(independent(units[:half * n_pairs]))
    _emit(independent(units[half * n_pairs:]), chain(range(half)))
    _emit(chain(range(half, n_chunks)))
    for p in pairs:
        st_ref[p] = S[p]

    for u in units:
        u["y"] = u["sd"][C:] + _mm(u["a_rb"], _bd_rows(u["P"], first))
    sums = _head_sums([u["y"] for u in units] + [u["rh"] * u["kh"] * rk_ref[:, u["sl"]] for u in units],
                      ones_bd)
    for u, s_y, s_rk in zip(units, sums[:len(units)], sums[len(units):]):
        u["yc"] = u["y"] - s_y * inv_dh
        u["bonus"] = s_rk * u["vh"]
    for u, s_var in zip(units, _head_sums([u["yc"] * u["yc"] for u in units], ones_bd)):
        sl = u["sl"]
        rows = slice(u["c"] * C, (u["c"] + 1) * C)
        yn = u["yc"] * lax.rsqrt(s_var * inv_dh + RWKV_GN_EPS) * lng_ref[:, sl] + lnb_ref[:, sl]
        o_ref[0, rows, sl] = ((yn + u["bonus"]) * g_ref[0, rows, sl].astype(F32)).astype(o_ref.dtype)


def _rwkv_call(r, w, k, v, a, g, k_k, k_a, r_k, ln_gain, ln_bias):
    B, S, D = r.shape
    H, Dh = RWKV_HEADS, RWKV_HEAD_DIM
    C = CHUNK * RWKV_CHUNKS_PER_STEP
    seq = pl.BlockSpec((1, C, D), lambda b, c: (b, c, 0))
    vec = pl.BlockSpec((1, D), lambda b, c: (0, 0))
    flat = lambda t: t.reshape(1, D)
    return pl.pallas_call(
        _rwkv_kernel,
        grid=(B, S // C),
        in_specs=[seq] * 6 + [vec] * 5,
        out_specs=seq,
        out_shape=jax.ShapeDtypeStruct((B, S, D), BF16),
        scratch_shapes=[pltpu.VMEM((H // 2, 2 * Dh, 2 * Dh), F32)],
        compiler_params=_params("parallel", "arbitrary"),
        name="rwkv",
    )(r, w, k, v, a, g, flat(k_k), flat(k_a), flat(r_k), flat(ln_gain), flat(ln_bias))


def kernel(x, c, positions, w_mod, b_mod, w_in0, gdn_conv_w, gdn_a_log, gdn_dt_bias, gdn_norm_gain, mla_q_norm_gain, mla_kv_norm_gain, mla_w_uq, mla_w_ukv, mla_q_head_gain, mla_k_head_gain, w_out0, rwkv_mu, rwkv_w_r, rwkv_w_k, rwkv_w_v, rwkv_w_o, rwkv_w0, rwkv_w1, rwkv_w2, rwkv_a0, rwkv_a1, rwkv_a2, rwkv_g1, rwkv_g2, rwkv_k_k, rwkv_k_a, rwkv_r_k, rwkv_ln_gain, rwkv_ln_bias, w_up, w_down):
    B, S, D = x.shape
    mod = _mod_call(c, w_mod, b_mod)

    def mod_chunks(layer):
        return [mod[layer, :, i * D:(i + 1) * D].reshape(B, 1, D) for i in range(6)]

    shift1, scale1, gate1, shift2, scale2, gate2 = mod_chunks(0)
    qkvz, lat = _inproj_call(x, shift1, scale1, _pack_w_in(w_in0[0]))
    o_gdn = _gdn_call(qkvz, lat, gdn_conv_w[0], gdn_a_log[0], gdn_dt_bias[0], gdn_norm_gain[0])
    q, k, v = _mla_prep_call(lat, positions, mla_q_norm_gain[0], mla_kv_norm_gain[0], mla_w_uq[0],
                             mla_w_ukv[0], mla_q_head_gain[0], mla_k_head_gain[0])
    o_mla = _flash_call(q, k, v)
    x = _mix_mlp_call([o_gdn, o_mla], x, gate1, shift2, scale2, gate2, w_out0[0].astype(BF16),
                      w_up[0].astype(BF16), w_down[0].astype(BF16))

    shift1, scale1, gate1, shift2, scale2, gate2 = mod_chunks(1)
    r, w, k, v, a, g = _rwkv_proj_call(x, shift1, scale1, rwkv_mu[0], rwkv_w_r[0], rwkv_w_k[0],
                                       rwkv_w_v[0], rwkv_w1[0], rwkv_w2[0], rwkv_a1[0], rwkv_a2[0],
                                       rwkv_g1[0], rwkv_g2[0], rwkv_w0[0], rwkv_a0[0])
    y = _rwkv_call(r, w, k, v, a, g, rwkv_k_k[0], rwkv_k_a[0], rwkv_r_k[0], rwkv_ln_gain[0],
                   rwkv_ln_bias[0])
    x = _mix_mlp_call([y], x, gate1, shift2, scale2, gate2, rwkv_w_o[0].astype(BF16),
                      w_up[1].astype(BF16), w_down[1].astype(BF16))
    return x
```

```python
import functools
import math

import jax
import jax.numpy as jnp
from jax import lax
from jax.experimental import pallas as pl
from jax.experimental.pallas import tpu as pltpu

F32 = jnp.float32
BF16 = jnp.bfloat16

D_MODEL = 1024
DEPTH = 2
GDN_HEADS = 8
GDN_HEAD_DIM = 64
GDN_WIDTH = GDN_HEADS * GDN_HEAD_DIM
GDN_CONV = 4
CHUNK = 64
GDN_CHUNKS_PER_STEP = 8
RWKV_CHUNKS_PER_STEP = 4
FLASH_TQ = 1024
FLASH_TK = 256
FLASH_UNROLL = (8, 4, 1)
MLA_HEADS = 8
MLA_Q_RANK = 256
MLA_KV_RANK = 128
MLA_NOPE = 64
MLA_ROPE = 32
MLA_QK = MLA_NOPE + MLA_ROPE
MLA_V = 64
ROPE_THETA = 10000.0
NEG_INF = -1e30
RWKV_HEADS = 16
RWKV_HEAD_DIM = 64
RWKV_GN_EPS = 64e-5
D_FF = 4 * D_MODEL
EPS = 1e-6
LANES = 128
SUBLANES = 8
VMEM_LIMIT = 56 * 1024 * 1024


def _params(*sem):
    return pltpu.CompilerParams(dimension_semantics=sem, vmem_limit_bytes=VMEM_LIMIT)


def _mm(a, b):
    return jnp.dot(a.astype(BF16), b.astype(BF16), preferred_element_type=F32)


def _mm_nt(a, b):
    return lax.dot_general(a.astype(BF16), b.astype(BF16), (((1,), (1,)), ((), ())),
                           preferred_element_type=F32)


def _tri_cumsum(tri_bf16, x):
    hi = x.astype(BF16)
    r1 = x - hi.astype(F32)
    mid = r1.astype(BF16)
    lo = (r1 - mid.astype(F32)).astype(BF16)
    dot = functools.partial(jnp.dot, preferred_element_type=F32)
    return dot(tri_bf16, hi) + dot(tri_bf16, mid) + dot(tri_bf16, lo)


def _unit_lower_inv(Ls, eye):
    n = Ls[0].shape[0]
    row = lax.broadcasted_iota(jnp.int32, (n, n), 0)
    col = lax.broadcasted_iota(jnp.int32, (n, n), 1)
    Ts = [eye] * len(Ls)
    s = 1
    while s < n:
        lower_left = ((row // (2 * s)) == (col // (2 * s))) & ((row // s) % 2 == 1) & ((col // s) % 2 == 0)
        Cs = [jnp.where(lower_left, L, 0.0) for L in Ls]
        if s == 1:
            Ts = [T - C for T, C in zip(Ts, Cs)]
        else:
            Xs = [_mm(T, C) for T, C in zip(Ts, Cs)]
            Ts = [T - _mm(X, T) for T, X in zip(Ts, Xs)]
        s *= 2
    return Ts


def _pair_masks(n):
    row = lax.broadcasted_iota(jnp.int32, (n, 2 * n), 0)
    lane = lax.broadcasted_iota(jnp.int32, (n, 2 * n), 1)
    col = lane % n
    return row >= col, row > col, (row == col).astype(F32), lane < n


def _bd_rows(x, first):
    x = x.astype(BF16)
    zero = jnp.zeros_like(x)
    return jnp.concatenate([jnp.where(first, x, zero), jnp.where(first, zero, x)], axis=0)


def _unit_lower_inv_pairs(units, eye_p, first):
    n = units[0]["L"].shape[0]
    row = lax.broadcasted_iota(jnp.int32, (n, 2 * n), 0)
    col = lax.broadcasted_iota(jnp.int32, (n, 2 * n), 1) % n
    Ts = [eye_p] * len(units)
    s = 1
    while s < n:
        lower_left = ((row // (2 * s)) == (col // (2 * s))) & ((row // s) % 2 == 1) & ((col // s) % 2 == 0)
        Cs = [jnp.where(lower_left, u["L"], 0.0) for u in units]
        if s == 1:
            Ts = [T - C for T, C in zip(Ts, Cs)]
        else:
            Xs = [_mm(T, _bd_rows(C, first)) for T, C in zip(Ts, Cs)]
            yield
            Ts = [T - _mm(X, _bd_rows(T, first)) for T, X in zip(Ts, Xs)]
            yield
        s *= 2
    for u, T in zip(units, Ts):
        u["T"] = T


def _emit(*gens):
    live = list(gens)
    while live:
        for g in list(live):
            try:
                next(g)
            except StopIteration:
                live.remove(g)


def _head_sums(xs, ones_bd, split=False):
    n = xs[0].shape[0]
    x = jnp.concatenate(xs, axis=0)
    hi = x.astype(BF16)
    out = jnp.dot(hi, ones_bd, preferred_element_type=F32)
    if split:
        out = out + jnp.dot((x - hi.astype(F32)).astype(BF16), ones_bd, preferred_element_type=F32)
    return [out[i * n:(i + 1) * n] for i in range(len(xs))]


def _softplus(x):
    return jnp.maximum(x, 0.0) + jnp.log1p(jnp.exp(-jnp.abs(x)))


def _silu(x):
    return x * jax.nn.sigmoid(x)


def _adaln(x, shift, scale):
    ms = jnp.mean(x * x, axis=-1, keepdims=True)
    return x * lax.rsqrt(ms + EPS) * (1.0 + scale) + shift


def _tri_masks(n):
    row = lax.broadcasted_iota(jnp.int32, (n, n), 0)
    col = lax.broadcasted_iota(jnp.int32, (n, n), 1)
    return row >= col, row > col, (row == col).astype(F32)


def _mod_kernel(c_ref, w_ref, b_ref, o_ref):
    cond = _silu(c_ref[...])
    o_ref[0] = jnp.dot(cond, w_ref[0], precision=lax.Precision.HIGHEST,
                       preferred_element_type=F32) + b_ref[0]


def _mod_call(c, w_mod, b_mod):
    B, D = c.shape
    depth, _, n = w_mod.shape
    tn = 1536
    c_pad = jnp.pad(c, ((0, SUBLANES - B), (0, 0)))
    out = pl.pallas_call(
        _mod_kernel,
        grid=(depth, n // tn),
        in_specs=[pl.BlockSpec((SUBLANES, D), lambda l, j: (0, 0)),
                  pl.BlockSpec((1, D, tn), lambda l, j: (l, 0, j)),
                  pl.BlockSpec((1, 1, tn), lambda l, j: (l, 0, j))],
        out_specs=pl.BlockSpec((1, SUBLANES, tn), lambda l, j: (l, 0, j)),
        out_shape=jax.ShapeDtypeStruct((depth, SUBLANES, n), F32),
        compiler_params=_params("parallel", "parallel"),
        name="mod",
    )(c_pad, w_mod, b_mod.reshape(depth, 1, n))
    return out[:, :B, :]


IN_PACKED = 2560


def _pack_w_in(w_in):
    D = w_in.shape[0]
    o_a = 4 * GDN_WIDTH
    o_dq = o_a + 2 * GDN_HEADS
    o_dkv = o_dq + MLA_Q_RANK
    o_kr = o_dkv + MLA_KV_RANK
    z = lambda n: jnp.zeros((D, n), w_in.dtype)
    return jnp.concatenate(
        [w_in[:, :o_a], w_in[:, o_dq:o_dkv], w_in[:, o_dkv:o_kr],
         w_in[:, o_a:o_dq], z(MLA_NOPE - 2 * GDN_HEADS), w_in[:, o_kr:o_kr + MLA_ROPE],
         z(LANES - MLA_NOPE - MLA_ROPE)], axis=1).astype(BF16)


def _inproj_kernel(x_ref, sh_ref, sc_ref, w_ref, qkvz_ref, lat_ref):
    h = _adaln(x_ref[0], sh_ref[0], sc_ref[0]).astype(BF16)
    y = jnp.dot(h, w_ref[...], preferred_element_type=F32)
    qkvz_ref[0] = y[:, :4 * GDN_WIDTH].astype(qkvz_ref.dtype)
    lat_ref[0] = y[:, 4 * GDN_WIDTH:]


def _inproj_call(x, shift, scale, w_packed):
    B, S, D = x.shape
    tm = min(1024, S)
    n_lat = IN_PACKED - 4 * GDN_WIDTH
    return pl.pallas_call(
        _inproj_kernel,
        grid=(B, S // tm),
        in_specs=[pl.BlockSpec((1, tm, D), lambda b, i: (b, i, 0)),
                  pl.BlockSpec((1, 1, D), lambda b, i: (b, 0, 0)),
                  pl.BlockSpec((1, 1, D), lambda b, i: (b, 0, 0)),
                  pl.BlockSpec((D, IN_PACKED), lambda b, i: (0, 0))],
        out_specs=[pl.BlockSpec((1, tm, 4 * GDN_WIDTH), lambda b, i: (b, i, 0)),
                   pl.BlockSpec((1, tm, n_lat), lambda b, i: (b, i, 0))],
        out_shape=[jax.ShapeDtypeStruct((B, S, 4 * GDN_WIDTH), BF16),
                   jax.ShapeDtypeStruct((B, S, n_lat), F32)],
        compiler_params=_params("parallel", "parallel"),
        name="inproj",
    )(x, shift, scale, w_packed)


def _gdn_kernel(x_ref, ab_ref, cw_ref, alog_ref, dtb_ref, gain_ref, o_ref, ext_ref, st_ref):
    C, H, Dh, W = CHUNK, GDN_HEADS, GDN_HEAD_DIM, GDN_WIDTH

    @pl.when(pl.program_id(1) == 0)
    def _init():
        ext_ref[0:SUBLANES, :] = jnp.zeros((SUBLANES, 3 * W), F32)
        st_ref[...] = jnp.zeros_like(st_ref)

    R = x_ref.shape[1]
    x = x_ref[0].astype(F32)
    ext_ref[SUBLANES:SUBLANES + R, :] = x[:, :3 * W]
    base = SUBLANES - (GDN_CONV - 1)
    conv = cw_ref[0:1, :] * ext_ref[base:base + R, :]
    for j in range(1, GDN_CONV):
        conv = conv + cw_ref[j:j + 1, :] * ext_ref[base + j:base + j + R, :]
    ext_ref[0:SUBLANES, :] = x[R - SUBLANES:R, :3 * W]
    qkv = _silu(conv)

    W2 = 2 * Dh
    pairs = range(H // 2)
    causal, strict, eye_p, first = _pair_masks(C)
    tri = _tri_masks(C)[0].astype(BF16)
    r2 = lax.broadcasted_iota(jnp.int32, (W2, W2), 0)
    c2 = lax.broadcasted_iota(jnp.int32, (W2, W2), 1)
    same_head = (r2 // Dh) == (c2 // Dh)
    ones_bd = same_head.astype(BF16)
    eye2 = (r2 == c2).astype(BF16)
    first_row = first[0:1, :]
    ab = ab_ref[0]
    beta_all = jax.nn.sigmoid(ab)
    g_all = -jnp.exp(alog_ref[...]) * _softplus(ab + dtb_ref[...])
    qk_slabs = [qkv[:, off + p * W2:off + (p + 1) * W2] for off in (0, W) for p in pairs]
    qk_norm = [t * lax.rsqrt(ss + 1e-12)
               for t, ss in zip(qk_slabs, _head_sums([t * t for t in qk_slabs], ones_bd, split=True))]
    q_n, k_n = qk_norm[:len(pairs)], qk_norm[len(pairs):]

    units = []
    for c in range(R // C):
        rows = slice(c * C, (c + 1) * C)
        gcum = _tri_cumsum(tri, g_all[rows])
        gcum_t = gcum.T
        eg = jnp.exp(gcum)
        glast = gcum[C - 1:C, :]
        eg_rest = jnp.exp(glast - gcum)
        eg_last = jnp.exp(glast)
        for p in pairs:
            h0, h1 = 2 * p, 2 * p + 1
            per_head = lambda t, off=0: jnp.where(first, t[:, off + h0:off + h0 + 1], t[:, off + h1:off + h1 + 1])
            q = q_n[p][rows] * (Dh ** -0.5)
            k = k_n[p][rows]
            bc = per_head(beta_all[rows], H)
            eg_p = per_head(eg)
            kb = k * bc
            g_col = jnp.concatenate([gcum_t[h0:h0 + 1, :], gcum_t[h1:h1 + 1, :]], axis=1)
            decay = jnp.where(causal, jnp.exp(jnp.where(causal, per_head(gcum) - g_col, 0.0)), 0.0)
            units.append(dict(
                c=c, p=p, decay=decay, vb=qkv[rows, 2 * W + p * W2:2 * W + (p + 1) * W2] * bc, k=k,
                kbq=jnp.concatenate([kb, q], axis=0),
                kb_g=kb * eg_p, q_g=q * eg_p,
                k_rest=k * per_head(eg_rest),
                e_last=jnp.where(first_row, eg_last[:, h0:h0 + 1], eg_last[:, h1:h1 + 1])))
    def independent(us):
        if not us:
            return
        for u in us:
            kk = _mm_nt(u["kbq"], _bd_rows(u["k"], first))
            u["L"] = jnp.where(strict, kk[:C] * u["decay"], 0.0)
            u["qk"] = jnp.where(causal, kk[C:] * u["decay"], 0.0)
            u["k_rest_t"] = _mm_nt(eye2, u["k_rest"])
        yield
        yield from _unit_lower_inv_pairs(us, eye_p, first)
        yield
        for u in us:
            u["u"] = _mm(u["T"], _bd_rows(u["vb"], first))
            w = _mm(u["T"], _bd_rows(u["kb_g"], first))
            u["wq_g"] = jnp.concatenate([w, u["q_g"]], axis=0)

    n_pairs = len(pairs)
    S = [st_ref[p] for p in pairs]

    def chain(chunks):
        for c in chunks:
            cu = units[c * n_pairs:(c + 1) * n_pairs]
            for u in cu:
                u["ks"] = _mm(u["wq_g"], S[u["p"]])
                u["v_new"] = u["u"] - u["ks"][:C]
            yield
            for u in cu:
                S[u["p"]] = (S[u["p"]] * u["e_last"]
                             + jnp.where(same_head, _mm(u["k_rest_t"], u["v_new"]), 0.0))
            yield

    n_chunks = R // C
    half = max(1, n_chunks // 2)
    _emit(independent(units[:half * n_pairs]))
    _emit(independent(units[half * n_pairs:]), chain(range(half)))
    _emit(chain(range(half, n_chunks)))
    for p in pairs:
        st_ref[p] = S[p]

    for u in units:
        u["o"] = u["ks"][C:] + _mm(u["qk"], _bd_rows(u["v_new"], first))
    for u, ss in zip(units, _head_sums([u["o"] * u["o"] for u in units], ones_bd)):
        rows = slice(u["c"] * C, (u["c"] + 1) * C)
        sl = slice(u["p"] * W2, (u["p"] + 1) * W2)
        o = u["o"] * lax.rsqrt(ss * (1.0 / Dh) + EPS)
        z = x[rows, 3 * W + u["p"] * W2:3 * W + (u["p"] + 1) * W2]
        o_ref[0, rows, sl] = (o * gain_ref[...] * _silu(z)).astype(o_ref.dtype)


def _gdn_call(qkvz, lat, conv_w, a_log, dt_bias, gain):
    B, S, _ = qkvz.shape
    H, Dh, W = GDN_HEADS, GDN_HEAD_DIM, GDN_WIDTH
    C = CHUNK * GDN_CHUNKS_PER_STEP
    lane_pad = lambda t: jnp.pad(t.reshape(1, H), ((0, 0), (0, LANES - H)))
    ab_block = (lat.shape[-1] // LANES) - 1
    return pl.pallas_call(
        _gdn_kernel,
        grid=(B, S // C),
        in_specs=[pl.BlockSpec((1, C, 4 * W), lambda b, c: (b, c, 0)),
                  pl.BlockSpec((1, C, LANES), lambda b, c: (b, c, ab_block)),
                  pl.BlockSpec((GDN_CONV, 3 * W), lambda b, c: (0, 0)),
                  pl.BlockSpec((1, LANES), lambda b, c: (0, 0)),
                  pl.BlockSpec((1, LANES), lambda b, c: (0, 0)),
                  pl.BlockSpec((1, 2 * Dh), lambda b, c: (0, 0))],
        out_specs=pl.BlockSpec((1, C, W), lambda b, c: (b, c, 0)),
        out_shape=jax.ShapeDtypeStruct((B, S, W), BF16),
        scratch_shapes=[pltpu.VMEM((C + SUBLANES, 3 * W), F32),
                        pltpu.VMEM((H // 2, 2 * Dh, 2 * Dh), F32)],
        compiler_params=_params("parallel", "arbitrary"),
        name="gdn",
    )(qkvz, lat, conv_w, lane_pad(a_log), lane_pad(dt_bias), jnp.tile(gain.reshape(1, Dh), (1, 2)))


def _mla_prep_kernel(lat_ref, pos_ref, invf_ref, qg_ref, kvg_ref, wuq_ref, wuk_ref, wuv_ref,
                     qhg_ref, khg_ref, q_ref, k_ref, v_ref):
    lat = lat_ref[0]
    dq = lat[:, :MLA_Q_RANK]
    dkv = lat[:, MLA_Q_RANK:MLA_Q_RANK + MLA_KV_RANK]
    slab = lat[:, MLA_Q_RANK + MLA_KV_RANK:]
    q_lat = dq * lax.rsqrt(jnp.mean(dq * dq, axis=-1, keepdims=True) + EPS) * qg_ref[...]
    kv_lat = dkv * lax.rsqrt(jnp.mean(dkv * dkv, axis=-1, keepdims=True) + EPS) * kvg_ref[...]
    qm = _mm(q_lat, wuq_ref[...])
    kn = _mm(kv_lat, wuk_ref[...])
    v_row = lax.broadcasted_iota(jnp.int32, (MLA_HEADS * LANES, 1), 0)
    v_ones = ((v_row % LANES) >= MLA_V).astype(F32)
    v_ref[0] = (_mm_nt(wuv_ref[...], kv_lat) + v_ones).astype(BF16)

    lane = lax.broadcasted_iota(jnp.int32, slab.shape, 1)
    half = MLA_ROPE // 2
    ang = pos_ref[0] * invf_ref[...]
    cosf = jnp.cos(ang)
    sinf = jnp.sin(ang)
    sin_signed = jnp.where(lane < MLA_NOPE + half, -sinf, sinf)
    k_rope = jnp.where((lane >= MLA_NOPE) & (lane < MLA_QK), slab, 0.0)

    def head_norm_rope(t, gain):
        t = t * lax.rsqrt(jnp.sum(t * t, axis=-1, keepdims=True) * (1.0 / MLA_QK) + EPS) * gain
        partner = jnp.where(lane < MLA_NOPE + half,
                            pltpu.roll(t, LANES - half, 1), pltpu.roll(t, half, 1))
        return t * cosf + partner * sin_signed

    scale = MLA_QK ** -0.5 * math.log2(math.e)
    for h in range(MLA_HEADS):
        sl = slice(h * LANES, (h + 1) * LANES)
        q_ref[0, :, sl] = (head_norm_rope(qm[:, sl], qhg_ref[...]) * scale).astype(BF16)
        k_ref[0, :, sl] = head_norm_rope(kn[:, sl] + k_rope, khg_ref[...]).astype(BF16)


def _mla_prep_call(lat, positions, q_norm_gain, kv_norm_gain, w_uq, w_ukv, q_head_gain, k_head_gain):
    B, S, n_lat = lat.shape
    H = MLA_HEADS
    tm = min(512, S)
    wuq = jnp.pad(w_uq.reshape(MLA_Q_RANK, H, MLA_QK), ((0, 0), (0, 0), (0, LANES - MLA_QK)))
    wuq = wuq.reshape(MLA_Q_RANK, H * LANES).astype(BF16)
    wkv = w_ukv.reshape(MLA_KV_RANK, H, MLA_NOPE + MLA_V)
    wuk = jnp.pad(wkv[:, :, :MLA_NOPE], ((0, 0), (0, 0), (0, LANES - MLA_NOPE)))
    wuk = wuk.reshape(MLA_KV_RANK, H * LANES).astype(BF16)
    wuv = jnp.pad(wkv[:, :, MLA_NOPE:], ((0, 0), (0, 0), (0, LANES - MLA_V)))
    wuv = wuv.reshape(MLA_KV_RANK, H * LANES).T.astype(BF16)
    pad_gain = lambda g: jnp.pad(g.reshape(1, MLA_QK), ((0, 0), (0, LANES - MLA_QK)))
    inv_freq = ROPE_THETA ** (-jnp.arange(0, MLA_ROPE, 2, dtype=F32) / MLA_ROPE)
    invf = jnp.concatenate([jnp.zeros((MLA_NOPE,), F32), inv_freq, inv_freq,
                            jnp.zeros((LANES - MLA_QK,), F32)]).reshape(1, LANES)
    pos = positions.astype(F32).reshape(B, S, 1)
    const = lambda shape: pl.BlockSpec(shape, lambda b, i: (0,) * len(shape))
    return pl.pallas_call(
        _mla_prep_kernel,
        grid=(B, S // tm),
        in_specs=[pl.BlockSpec((1, tm, n_lat), lambda b, i: (b, i, 0)),
                  pl.BlockSpec((1, tm, 1), lambda b, i: (b, i, 0)),
                  const((1, LANES)), const((1, MLA_Q_RANK)), const((1, MLA_KV_RANK)),
                  const((MLA_Q_RANK, H * LANES)), const((MLA_KV_RANK, H * LANES)),
                  const((H * LANES, MLA_KV_RANK)), const((1, LANES)), const((1, LANES))],
        out_specs=[pl.BlockSpec((1, tm, H * LANES), lambda b, i: (b, i, 0)),
                   pl.BlockSpec((1, tm, H * LANES), lambda b, i: (b, i, 0)),
                   pl.BlockSpec((1, H * LANES, tm), lambda b, i: (b, 0, i))],
        out_shape=[jax.ShapeDtypeStruct((B, S, H * LANES), BF16)] * 2
        + [jax.ShapeDtypeStruct((B, H * LANES, S), BF16)],
        compiler_params=_params("parallel", "parallel"),
        name="mla_prep",
    )(lat, pos, invf, q_norm_gain.reshape(1, -1), kv_norm_gain.reshape(1, -1), wuq, wuk, wuv,
      pad_gain(q_head_gain), pad_gain(k_head_gain))


def _flash_kernel(q_ref, k_ref, vt_ref, o_ref, *, tq, tk):
    qi = pl.program_id(2)
    q = q_ref[0]
    q0, q1 = q[:, :LANES], q[:, LANES:]
    key = lax.broadcasted_iota(jnp.int32, (tk, tq), 0)
    qry = lax.broadcasted_iota(jnp.int32, (tk, tq), 1)

    def scores(kj, c0=0):
        start = pl.multiple_of(kj * tk, tk)
        rows = pl.ds(start, tk)
        return (_mm_nt(k_ref[0, rows, :LANES], q0[c0:]),
                _mm_nt(k_ref[0, rows, LANES:], q1[c0:]))

    def online(st, m, acc, vt):
        m_new = jnp.maximum(m, jnp.max(st, axis=0, keepdims=True))
        pt = jnp.exp2(st - m_new)
        return m_new, jnp.exp2(m - m_new) * acc + _mm(vt, pt)

    def softmax_pv(kj, s0, s1, m0, m1, acc0, acc1):
        start = pl.multiple_of(kj * tk, tk)
        cols = pl.ds(start, tk)
        m0, acc0 = online(s0, m0, acc0, vt_ref[0, :LANES, cols])
        m1, acc1 = online(s1, m1, acc1, vt_ref[0, LANES:, cols])
        return m0, m1, acc0, acc1

    def group(size, first_block):
        def body(t, carry):
            blocks = [first_block + size * t + u for u in range(size)]
            ss = [scores(kj) for kj in blocks]
            for kj, s in zip(blocks, ss):
                carry = softmax_pv(kj, *s, *carry)
            return carry
        return body

    neg = jnp.full((1, tq), NEG_INF, F32)
    zero = jnp.zeros((LANES, tq), F32)
    n_full = qi * (tq // tk)
    carry = (neg, neg, zero, zero)
    done = 0
    for size in FLASH_UNROLL:
        trips = (n_full - done) // size
        carry = lax.fori_loop(0, trips, group(size, done), carry)
        done = done + trips * size
    m0, m1, acc0, acc1 = carry
    for d in range(tq // tk):
        c0 = d * tk
        s0, s1 = scores(n_full + d, c0)
        mask = (key <= qry)[:, :tq - c0]
        sub = softmax_pv(n_full + d, jnp.where(mask, s0, NEG_INF), jnp.where(mask, s1, NEG_INF),
                         m0[:, c0:], m1[:, c0:], acc0[:, c0:], acc1[:, c0:])
        m0, m1, acc0, acc1 = (jnp.concatenate([old[:, :c0], new], axis=1) if c0 else new
                              for old, new in zip((m0, m1, acc0, acc1), sub))
    out_t = jnp.concatenate([acc0[:MLA_V] / acc0[MLA_V:MLA_V + 1],
                             acc1[:MLA_V] / acc1[MLA_V:MLA_V + 1]], axis=0)
    o_ref[0] = out_t.T.astype(o_ref.dtype)


def _flash_call(q, k, v):
    B, S, _ = q.shape
    tq = min(FLASH_TQ, S)
    pairs = MLA_HEADS // 2
    return pl.pallas_call(
        functools.partial(_flash_kernel, tq=tq, tk=min(FLASH_TK, tq)),
        grid=(B, pairs, S // tq),
        in_specs=[pl.BlockSpec((1, tq, 2 * LANES), lambda b, p, i: (b, i, p)),
                  pl.BlockSpec((1, S, 2 * LANES), lambda b, p, i: (b, 0, p)),
                  pl.BlockSpec((1, 2 * LANES, S), lambda b, p, i: (b, p, 0))],
        out_specs=pl.BlockSpec((1, tq, 2 * MLA_V), lambda b, p, i: (b, i, p)),
        out_shape=jax.ShapeDtypeStruct((B, S, MLA_HEADS * MLA_V), BF16),
        compiler_params=_params("parallel", "parallel", "arbitrary"),
        name="flash",
    )(q, k, v)


def _mix_mlp_kernel(*refs, n_in):
    ins = refs[:n_in]
    (x_ref, g1_ref, sh_ref, sc_ref, g2_ref, wo_ref, wu_ref, wd_ref, o_ref,
     x1_ref, h_ref, acc_ref) = refs[n_in:]
    f = pl.program_id(2)

    @pl.when(f == 0)
    def _first():
        y = None
        off = 0
        for r in ins:
            n = r.shape[-1]
            part = _mm(r[0], wo_ref[off:off + n, :])
            y = part if y is None else y + part
            off += n
        x1 = x_ref[0] + g1_ref[0] * y
        x1_ref[...] = x1
        h_ref[...] = _adaln(x1, sh_ref[0], sc_ref[0]).astype(BF16)
        acc_ref[...] = jnp.zeros_like(acc_ref)

    u = jnp.maximum(jnp.dot(h_ref[...], wu_ref[...], preferred_element_type=F32), 0.0)
    acc_ref[...] += _mm(u * u, wd_ref[...])

    @pl.when(f == pl.num_programs(2) - 1)
    def _last():
        o_ref[0] = x1_ref[...] + g2_ref[0] * acc_ref[...]


def _mix_mlp_call(ins, x, gate1, shift, scale, gate2, w_out, w_up, w_down):
    B, S, D = x.shape
    F = w_up.shape[1]
    tm = min(1024, S)
    tf = 1024
    n_in = len(ins)
    vec = pl.BlockSpec((1, 1, D), lambda b, i, f: (b, 0, 0))
    rows = lambda n: pl.BlockSpec((1, tm, n), lambda b, i, f: (b, i, 0))
    return pl.pallas_call(
        functools.partial(_mix_mlp_kernel, n_in=n_in),
        grid=(B, S // tm, F // tf),
        in_specs=[rows(a.shape[-1]) for a in ins] + [rows(D), vec, vec, vec, vec,
                  pl.BlockSpec(w_out.shape, lambda b, i, f: (0, 0)),
                  pl.BlockSpec((D, tf), lambda b, i, f: (0, f)),
                  pl.BlockSpec((tf, D), lambda b, i, f: (f, 0))],
        out_specs=rows(D),
        out_shape=jax.ShapeDtypeStruct((B, S, D), F32),
        scratch_shapes=[pltpu.VMEM((tm, D), F32), pltpu.VMEM((tm, D), BF16), pltpu.VMEM((tm, D), F32)],
        compiler_params=_params("parallel", "parallel", "arbitrary"),
        name="mix_mlp",
    )(*ins, x, gate1, shift, scale, gate2, w_out, w_up, w_down)


def _rwkv_proj_kernel(x_ref, xp_ref, sh_ref, sc_ref, mu_ref, wr_ref, wk_ref, wv_ref,
                      w1_ref, w2_ref, a1_ref, a2_ref, g1_ref, g2_ref, w0_ref, a0_ref,
                      r_ref, w_ref, k_ref, v_ref, a_ref, g_ref):
    h = _adaln(x_ref[0], sh_ref[0], sc_ref[0])
    tm = h.shape[0]
    prev_tail = _adaln(xp_ref[0], sh_ref[0], sc_ref[0])[SUBLANES - 1:SUBLANES, :]
    prev_tail = jnp.where(pl.program_id(1) == 0, 0.0, prev_tail)
    row = lax.broadcasted_iota(jnp.int32, h.shape, 0)
    h_prev = jnp.where(row == 0, prev_tail, pltpu.roll(h, 1, 0))
    xx = h_prev - h
    mix = lambda j: h + xx * mu_ref[j:j + 1, :]
    xr, xw, xk, xv, xa, xg = (mix(j) for j in range(6))
    r_ref[0] = _mm(xr, wr_ref[...]).astype(r_ref.dtype)
    k_ref[0] = _mm(xk, wk_ref[...]).astype(k_ref.dtype)
    v_ref[0] = _mm(xv, wv_ref[...]).astype(v_ref.dtype)
    wl = w0_ref[...] + _mm(jnp.tanh(_mm(xw, w1_ref[...])), w2_ref[...])
    w_log = -_softplus(-wl) - 0.5
    w_ref[0] = -jnp.exp(w_log)
    a_ref[0] = jax.nn.sigmoid(a0_ref[...] + _mm(_mm(xa, a1_ref[...]), a2_ref[...])).astype(a_ref.dtype)
    g_ref[0] = _mm(jax.nn.sigmoid(_mm(xg, g1_ref[...])), g2_ref[...]).astype(g_ref.dtype)


def _pad_lora(w_a, w_b):
    r = w_a.shape[1]
    rp = -(-r // LANES) * LANES
    return (jnp.pad(w_a, ((0, 0), (0, rp - r))).astype(BF16),
            jnp.pad(w_b, ((0, rp - r), (0, 0))).astype(BF16))


def _rwkv_proj_call(x, shift, scale, mu, w_r, w_k, w_v, w1, w2, a1, a2, g1, g2, w0, a0):
    B, S, D = x.shape
    tm = min(512, S)
    w1p, w2p = _pad_lora(w1, w2)
    a1p, a2p = _pad_lora(a1, a2)
    g1p, g2p = _pad_lora(g1, g2)
    const = lambda t: pl.BlockSpec(t.shape, lambda b, i: (0,) * t.ndim)
    vec = pl.BlockSpec((1, 1, D), lambda b, i: (b, 0, 0))
    row_blocks = tm // SUBLANES
    weights = (mu, w_r.astype(BF16), w_k.astype(BF16), w_v.astype(BF16), w1p, w2p, a1p, a2p,
               g1p, g2p, w0.reshape(1, D), a0.reshape(1, D))
    out = pl.BlockSpec((1, tm, D), lambda b, i: (b, i, 0))
    return pl.pallas_call(
        _rwkv_proj_kernel,
        grid=(B, S // tm),
        in_specs=[pl.BlockSpec((1, tm, D), lambda b, i: (b, i, 0)),
                  pl.BlockSpec((1, SUBLANES, D),
                               lambda b, i: (b, jnp.maximum(i * row_blocks - 1, 0), 0)),
                  vec, vec] + [const(t) for t in weights],
        out_specs=[out] * 6,
        out_shape=[jax.ShapeDtypeStruct((B, S, D), F32 if i == 1 else BF16) for i in range(6)],
        compiler_params=_params("parallel", "parallel"),
        name="rwkv_proj",
    )(x, x, shift, scale, *weights)


def _rwkv_kernel(r_ref, w_ref, k_ref, v_ref, a_ref, g_ref, kk_ref, ka_ref, rk_ref, lng_ref, lnb_ref,
                 o_ref, st_ref):
    C, Dh = CHUNK, RWKV_HEAD_DIM
    W2 = 2 * Dh
    pairs = range(RWKV_HEADS // 2)

    @pl.when(pl.program_id(1) == 0)
    def _init():
        st_ref[...] = jnp.zeros_like(st_ref)

    R = r_ref.shape[1]
    incl, strict, eye_p, first = _pair_masks(C)
    tri = _tri_masks(C)[0].astype(BF16)
    r2 = lax.broadcasted_iota(jnp.int32, (W2, W2), 0)
    c2 = lax.broadcasted_iota(jnp.int32, (W2, W2), 1)
    same_head = (r2 // Dh) == (c2 // Dh)
    ones_bd = same_head.astype(BF16)
    eye2 = (r2 == c2).astype(BF16)
    inv_dh = 1.0 / Dh

    units = []
    for c in range(R // C):
        rows = slice(c * C, (c + 1) * C)
        w = w_ref[0, rows, :]
        G = _tri_cumsum(tri, w)
        g_last = G[C - 1:C, :]
        e_in = jnp.exp(G)
        e_ex = jnp.exp(G - w)
        e_neg = jnp.exp(-G)
        e_rest = jnp.exp(g_last - G)
        e_last_col = jnp.broadcast_to(jnp.exp(g_last), (SUBLANES, g_last.shape[1])).T
        r = r_ref[0, rows, :].astype(F32)
        k0 = k_ref[0, rows, :].astype(F32)
        a = a_ref[0, rows, :].astype(F32)
        kk_raw = k0 * kk_ref[...]
        k_mod = k0 * (1.0 + (a - 1.0) * ka_ref[...])
        kk_slabs = [kk_raw[:, p * W2:(p + 1) * W2] for p in pairs]
        kk_sums = _head_sums([t * t for t in kk_slabs], ones_bd, split=True)
        for p in pairs:
            sl = slice(p * W2, (p + 1) * W2)
            kk = kk_slabs[p] * lax.rsqrt(kk_sums[p] + 1e-12)
            kh, rh, vh = k_mod[:, sl], r[:, sl], v_ref[0, rows, sl].astype(F32)
            bv = kk * a[:, sl]
            units.append(dict(
                c=c, sl=sl, rh=rh, kh=kh, vh=vh,
                ar=jnp.concatenate([-kk * e_ex[:, sl], rh * e_in[:, sl]], axis=0),
                b_t=bv * e_neg[:, sl], k_t=kh * e_neg[:, sl],
                bk_rest=jnp.concatenate([bv * e_rest[:, sl], kh * e_rest[:, sl]], axis=0),
                e_last=e_last_col[sl, 0:1]))
    def independent(us):
        if not us:
            return
        for u in us:
            ab = _mm_nt(u["ar"], _bd_rows(u["b_t"], first))
            ak = _mm_nt(u["ar"], _bd_rows(u["k_t"], first))
            u["L"] = -jnp.where(strict, ab[:C], 0.0)
            u["a_rb"] = jnp.where(incl, ab[C:], 0.0)
            u["akrk"] = jnp.concatenate([jnp.where(strict, ak[:C], 0.0), jnp.where(incl, ak[C:], 0.0)],
                                        axis=0)
            u["bk_rest_t"] = _mm_nt(eye2, u["bk_rest"])
        yield
        for u in us:
            u["kv"] = _mm(u["akrk"], _bd_rows(u["vh"], first))
        yield
        yield from _unit_lower_inv_pairs(us, eye_p, first)

    n_pairs = len(pairs)
    S = [st_ref[p] for p in pairs]

    def chain(chunks):
        for c in chunks:
            cu = units[c * n_pairs:(c + 1) * n_pairs]
            for p, u in zip(pairs, cu):
                u["sd"] = _mm(u["ar"], S[p]) + u["kv"]
            yield
            for u in cu:
                u["P"] = _mm(u["T"], _bd_rows(u["sd"][:C], first))
            yield
            for p, u in zip(pairs, cu):
                pv = jnp.concatenate([u["P"], u["vh"]], axis=0)
                S[p] = S[p] * u["e_last"] + jnp.where(same_head, _mm(u["bk_rest_t"], pv), 0.0)
            yield

    n_chunks = R // C
    half = max(1, n_chunks // 2)
    _emit(independent(units[:half * n_pairs]))
    _emit(independent(units[half * n_pairs:]), chain(range(half)))
    _emit(chain(range(half, n_chunks)))
    for p in pairs:
        st_ref[p] = S[p]

    for u in units:
        u["y"] = u["sd"][C:] + _mm(u["a_rb"], _bd_rows(u["P"], first))
    sums = _head_sums([u["y"] for u in units] + [u["rh"] * u["kh"] * rk_ref[:, u["sl"]] for u in units],
                      ones_bd)
    for u, s_y, s_rk in zip(units, sums[:len(units)], sums[len(units):]):
        u["yc"] = u["y"] - s_y * inv_dh
        u["bonus"] = s_rk * u["vh"]
    for u, s_var in zip(units, _head_sums([u["yc"] * u["yc"] for u in units], ones_bd)):
        sl = u["sl"]
        rows = slice(u["c"] * C, (u["c"] + 1) * C)
        yn = u["yc"] * lax.rsqrt(s_var * inv_dh + RWKV_GN_EPS) * lng_ref[:, sl] + lnb_ref[:, sl]
        o_ref[0, rows, sl] = ((yn + u["bonus"]) * g_ref[0, rows, sl].astype(F32)).astype(o_ref.dtype)


def _rwkv_call(r, w, k, v, a, g, k_k, k_a, r_k, ln_gain, ln_bias):
    B, S, D = r.shape
    H, Dh = RWKV_HEADS, RWKV_HEAD_DIM
    C = CHUNK * RWKV_CHUNKS_PER_STEP
    seq = pl.BlockSpec((1, C, D), lambda b, c: (b, c, 0))
    vec = pl.BlockSpec((1, D), lambda b, c: (0, 0))
    flat = lambda t: t.reshape(1, D)
    return pl.pallas_call(
        _rwkv_kernel,
        grid=(B, S // C),
        in_specs=[seq] * 6 + [vec] * 5,
        out_specs=seq,
        out_shape=jax.ShapeDtypeStruct((B, S, D), BF16),
        scratch_shapes=[pltpu.VMEM((H // 2, 2 * Dh, 2 * Dh), F32)],
        compiler_params=_params("parallel", "arbitrary"),
        name="rwkv",
    )(r, w, k, v, a, g, flat(k_k), flat(k_a), flat(r_k), flat(ln_gain), flat(ln_bias))


def kernel(x, c, positions, w_mod, b_mod, w_in0, gdn_conv_w, gdn_a_log, gdn_dt_bias, gdn_norm_gain, mla_q_norm_gain, mla_kv_norm_gain, mla_w_uq, mla_w_ukv, mla_q_head_gain, mla_k_head_gain, w_out0, rwkv_mu, rwkv_w_r, rwkv_w_k, rwkv_w_v, rwkv_w_o, rwkv_w0, rwkv_w1, rwkv_w2, rwkv_a0, rwkv_a1, rwkv_a2, rwkv_g1, rwkv_g2, rwkv_k_k, rwkv_k_a, rwkv_r_k, rwkv_ln_gain, rwkv_ln_bias, w_up, w_down):
    B, S, D = x.shape
    mod = _mod_call(c, w_mod, b_mod)

    def mod_chunks(layer):
        return [mod[layer, :, i * D:(i + 1) * D].reshape(B, 1, D) for i in range(6)]

    shift1, scale1, gate1, shift2, scale2, gate2 = mod_chunks(0)
    qkvz, lat = _inproj_call(x, shift1, scale1, _pack_w_in(w_in0[0]))
    o_gdn = _gdn_call(qkvz, lat, gdn_conv_w[0], gdn_a_log[0], gdn_dt_bias[0], gdn_norm_gain[0])
    q, k, v = _mla_prep_call(lat, positions, mla_q_norm_gain[0], mla_kv_norm_gain[0], mla_w_uq[0],
                             mla_w_ukv[0], mla_q_head_gain[0], mla_k_head_gain[0])
    o_mla = _flash_call(q, k, v)
    x = _mix_mlp_call([o_gdn, o_mla], x, gate1, shift2, scale2, gate2, w_out0[0].astype(BF16),
                      w_up[0].astype(BF16), w_down[0].astype(BF16))

    shift1, scale1, gate1, shift2, scale2, gate2 = mod_chunks(1)
    r, w, k, v, a, g = _rwkv_proj_call(x, shift1, scale1, rwkv_mu[0], rwkv_w_r[0], rwkv_w_k[0],
                                       rwkv_w_v[0], rwkv_w1[0], rwkv_w2[0], rwkv_a1[0], rwkv_a2[0],
                                       rwkv_g1[0], rwkv_g2[0], rwkv_w0[0], rwkv_a0[0])
    y = _rwkv_call(r, w, k, v, a, g, rwkv_k_k[0], rwkv_k_a[0], rwkv_r_k[0], rwkv_ln_gain[0],
                   rwkv_ln_bias[0])
    x = _mix_mlp_call([y], x, gate1, shift2, scale2, gate2, rwkv_w_o[0].astype(BF16),
                      w_up[1].astype(BF16), w_down[1].astype(BF16))
    return x
```
